```python
import jax, jax.numpy as jnp
from jax import lax
import numpy as np

D_MODEL = 2048
BATCH = 2
SEQ = 16384
DEPTH = 1
DEC_BATCH = 8
DEC_SEQ = 4096
PAST_LEN = 128

N_META = 16
N_HEADS = 16
QK_NOPE_DIM = 128
QK_ROPE_DIM = 64
QK_DIM = QK_NOPE_DIM + QK_ROPE_DIM
V_DIM = 128
Q_LORA_RANK = 512
KV_LORA_RANK = 512
ATTN_WIDTH = N_HEADS * V_DIM
CONV_WIDTH = D_MODEL
CONV_K = 3
Q_BLOCK = 128
ROPE_THETA = 10000.0
NORM_EPS = 1e-6
IN_SPLIT_SIZES = (Q_LORA_RANK, KV_LORA_RANK, QK_ROPE_DIM, ATTN_WIDTH,
                  CONV_WIDTH, CONV_WIDTH, CONV_WIDTH, CONV_WIDTH, 2 * D_MODEL)
IN_WIDTH = int(sum(IN_SPLIT_SIZES))
IN_SPLIT_POINTS = tuple(int(p) for p in np.cumsum(IN_SPLIT_SIZES)[:-1])

kernel_name = 'hybrid_mla_shortconv_encoder'


def rmsnorm(x, w):
    xf = x.astype(jnp.float32)
    y = xf * lax.rsqrt(jnp.mean(xf * xf, axis=-1, keepdims=True) + NORM_EPS)
    return (y * w.astype(jnp.float32)).astype(x.dtype)


def rope_tables(seq_len, dtype):
    inv_freq = 1.0 / (ROPE_THETA ** (jnp.arange(0, QK_ROPE_DIM, 2, dtype=jnp.float32) / QK_ROPE_DIM))
    ang = jnp.arange(seq_len, dtype=jnp.float32)[:, None] * inv_freq[None, :]
    return jnp.cos(ang).astype(dtype), jnp.sin(ang).astype(dtype)


def apply_rope(x, cos, sin):
    x1, x2 = jnp.split(x, 2, axis=-1)
    return jnp.concatenate([x1 * cos - x2 * sin, x2 * cos + x1 * sin], axis=-1)


def mla_attention(q_nope, q_rope, k_nope, k_rope, v):
    b, seq_len, h, _ = q_nope.shape
    n_blocks = -(-seq_len // Q_BLOCK)
    pad = n_blocks * Q_BLOCK - seq_len
    scale = QK_DIM ** -0.5

    def to_blocks(q):
        q = jnp.pad(q, ((0, 0), (0, pad), (0, 0), (0, 0)))
        return q.reshape(b, n_blocks, Q_BLOCK, h, q.shape[-1]).transpose(1, 0, 2, 3, 4)

    def block(qs):
        qn_b, qr_b = qs
        s = (jnp.einsum('bqhd,bkhd->bhqk', qn_b, k_nope)
             + jnp.einsum('bqhr,bkr->bhqk', qr_b, k_rope)).astype(jnp.float32) * scale
        p = jax.nn.softmax(s, axis=-1).astype(v.dtype)
        return jnp.einsum('bhqk,bkhv->bqhv', p, v)

    o = lax.map(block, (to_blocks(q_nope), to_blocks(q_rope)))
    o = o.transpose(1, 0, 2, 3, 4).reshape(b, n_blocks * Q_BLOCK, h * V_DIM)
    return o[:, :seq_len]


def hybrid_layer(x, norm_w, w_in, b_gate, q_a_norm_w, w_uq, kv_a_norm_w, w_ukv,
                 w_o_attn, conv_w, w_o_conv, w_o):
    b, seq_len, _ = x.shape
    xn = rmsnorm(x, norm_w)
    proj = xn @ w_in
    q_a, c_kv, k_rope, z_attn, cx, cb, cc, z_conv, g_logits = jnp.split(proj, IN_SPLIT_POINTS, axis=-1)

    q = (rmsnorm(q_a, q_a_norm_w) @ w_uq).reshape(b, seq_len, N_HEADS, QK_DIM)
    q_nope, q_rope = q[..., :QK_NOPE_DIM], q[..., QK_NOPE_DIM:]
    kv = (rmsnorm(c_kv, kv_a_norm_w) @ w_ukv).reshape(b, seq_len, N_HEADS, QK_NOPE_DIM + V_DIM)
    k_nope, v = kv[..., :QK_NOPE_DIM], kv[..., QK_NOPE_DIM:]
    cos, sin = rope_tables(seq_len, x.dtype)
    q_rope = apply_rope(q_rope, cos[:, None, :], sin[:, None, :])
    k_rope = apply_rope(k_rope, cos, sin)
    o = mla_attention(q_nope, q_rope, k_nope, k_rope, v)
    y_attn = (o * jax.nn.silu(z_attn)) @ w_o_attn

    u = cc * cx
    u_prev = jnp.pad(u, ((0, 0), (1, 0), (0, 0)))[:, :-1]
    u_next = jnp.pad(u, ((0, 0), (0, 1), (0, 0)))[:, 1:]
    conv = conv_w[0] * u_prev + conv_w[1] * u + conv_w[2] * u_next
    y_conv = (cb * conv * jax.nn.silu(z_conv)) @ w_o_conv

    g = jax.nn.sigmoid(g_logits + b_gate)
    g_attn, g_conv = g[..., :D_MODEL], g[..., D_MODEL:]
    merged = g_attn * y_attn + g_conv * y_conv
    return x + merged @ w_o


def trunk(x, meta_tokens, norm_w, w_in, b_gate, q_a_norm_w, w_uq, kv_a_norm_w, w_ukv,
          w_o_attn, conv_w, w_o_conv, w_o, final_norm_w):
    b = x.shape[0]
    meta = jnp.broadcast_to(meta_tokens.astype(x.dtype)[None], (b, N_META, D_MODEL))
    h = jnp.concatenate([meta, x], axis=1)
    for l in range(DEPTH):
        h = hybrid_layer(h, norm_w[l], w_in[l], b_gate[l], q_a_norm_w[l], w_uq[l],
                         kv_a_norm_w[l], w_ukv[l], w_o_attn[l], conv_w[l], w_o_conv[l], w_o[l])
    h = rmsnorm(h, final_norm_w)
    return h[:, N_META:]


def setup_inputs(seed: int = 0) -> dict:
    key = jax.random.key(seed)
    ks = jax.random.split(key, 16)
    f32 = jnp.float32
    nrm = lambda k, shape, s: jax.random.normal(k, shape, f32) * s
    return {
        'x_prompt': nrm(ks[0], (BATCH, SEQ, D_MODEL), 1.0),
        'x_sample': nrm(ks[1], (DEC_BATCH, DEC_SEQ, D_MODEL), 1.0),
        'meta_tokens': nrm(ks[2], (N_META, D_MODEL), 1.0),
        'norm_w': 1.0 + nrm(ks[3], (DEPTH, D_MODEL), 0.02),
        'w_in': nrm(ks[4], (DEPTH, D_MODEL, IN_WIDTH), D_MODEL ** -0.5),
        'b_gate': nrm(ks[5], (DEPTH, 2 * D_MODEL), 0.01),
        'q_a_norm_w': 1.0 + nrm(ks[6], (DEPTH, Q_LORA_RANK), 0.02),
        'w_uq': nrm(ks[7], (DEPTH, Q_LORA_RANK, N_HEADS * QK_DIM), Q_LORA_RANK ** -0.5),
        'kv_a_norm_w': 1.0 + nrm(ks[8], (DEPTH, KV_LORA_RANK), 0.02),
        'w_ukv': nrm(ks[9], (DEPTH, KV_LORA_RANK, N_HEADS * (QK_NOPE_DIM + V_DIM)), KV_LORA_RANK ** -0.5),
        'w_o_attn': nrm(ks[10], (DEPTH, ATTN_WIDTH, D_MODEL), ATTN_WIDTH ** -0.5),
        'conv_w': nrm(ks[11], (DEPTH, CONV_K, CONV_WIDTH), CONV_K ** -0.5),
        'w_o_conv': nrm(ks[12], (DEPTH, CONV_WIDTH, D_MODEL), CONV_WIDTH ** -0.5),
        'w_o': nrm(ks[13], (DEPTH, D_MODEL, D_MODEL), D_MODEL ** -0.5),
        'final_norm_w': 1.0 + nrm(ks[14], (D_MODEL,), 0.02),
    }


def reference(x_prompt, x_sample, meta_tokens, norm_w, w_in, b_gate, q_a_norm_w, w_uq,
              kv_a_norm_w, w_ukv, w_o_attn, conv_w, w_o_conv, w_o, final_norm_w):
    y_prompt = trunk(x_prompt, meta_tokens, norm_w, w_in, b_gate, q_a_norm_w, w_uq, kv_a_norm_w,
                     w_ukv, w_o_attn, conv_w, w_o_conv, w_o, final_norm_w)
    y_sample = trunk(x_sample, meta_tokens, norm_w, w_in, b_gate, q_a_norm_w, w_uq, kv_a_norm_w,
                     w_ukv, w_o_attn, conv_w, w_o_conv, w_o, final_norm_w)
    return (y_prompt, y_sample)
```

```python
import functools

import jax
import jax.numpy as jnp
from jax import lax
from jax.experimental import pallas as pl
from jax.experimental.pallas import tpu as pltpu

QK_NOPE_DIM = 128
QK_ROPE_DIM = 64
QK_DIM = QK_NOPE_DIM + QK_ROPE_DIM
V_DIM = 128
ROPE_HALF = QK_ROPE_DIM // 2
QK_PAD = 256
ROPE_THETA = 10000.0
NORM_EPS = 1e-6
MASK_VALUE = -1e30
META_PAD = 128
VMEM_LIMIT_BYTES = 56 * 1024 * 1024

F32 = jnp.float32
BF16 = jnp.bfloat16


def _params(*semantics):
    return pltpu.CompilerParams(dimension_semantics=semantics,
                                vmem_limit_bytes=VMEM_LIMIT_BYTES)


def _tile(n, pref):
    if n <= pref:
        return n
    t = pref
    while n % t:
        t //= 2
    return t


def _nt_dot(a, b):
    return lax.dot_general(a, b, (((1,), (1,)), ((), ())), preferred_element_type=F32)


def _rms(x, w):
    return x * lax.rsqrt(jnp.mean(x * x, axis=-1, keepdims=True) + NORM_EPS) * w


def _resident(shape):
    nd = len(shape)
    return pl.BlockSpec(shape, lambda *_: (0,) * nd, pipeline_mode=pl.Buffered(1))


def _rmsnorm_kernel(x_ref, w_ref, o_ref):
    o_ref[...] = _rms(x_ref[...], w_ref[...]).astype(o_ref.dtype)


def _rmsnorm(x, w):
    t, d = x.shape
    tm = _tile(t, 512)
    return pl.pallas_call(
        _rmsnorm_kernel,
        out_shape=jax.ShapeDtypeStruct((t, d), BF16),
        grid=(t // tm,),
        in_specs=[pl.BlockSpec((tm, d), lambda i: (i, 0)),
                  pl.BlockSpec((1, d), lambda i: (0, 0))],
        out_specs=pl.BlockSpec((tm, d), lambda i: (i, 0)),
        compiler_params=_params("parallel"),
        name="rmsnorm",
    )(x, w.reshape(1, d))


def _attn_prep_kernel(xn_ref, wsm_ref, qnw_ref, kvnw_ref, wuqT_ref, wuk_ref, wuvT_ref,
                      cosT_ref, sinT_ref, cosK_ref, sinK_ref,
                      qT_ref, k_ref, vT_ref, *, n_heads, q_rank, kv_rank, scale):
    tm = xn_ref.shape[1]
    p = jnp.dot(xn_ref[0], wsm_ref[...], preferred_element_type=F32)
    qn = _rms(p[:, :q_rank], qnw_ref[...]).astype(BF16)
    cn = _rms(p[:, q_rank:q_rank + kv_rank], kvnw_ref[...]).astype(BF16)
    kr = p[:, q_rank + kv_rank:]
    kr_sw = jnp.concatenate([kr[:, ROPE_HALF:], kr[:, :ROPE_HALF]], axis=1)
    kr_rot = (kr * cosK_ref[...] + kr_sw * sinK_ref[...]).astype(BF16)

    qT = _nt_dot(wuqT_ref[...], qn) * scale
    k_nope = jnp.dot(cn, wuk_ref[...], preferred_element_type=F32)
    vT = _nt_dot(wuvT_ref[...], cn)
    c = cosT_ref[...]
    s = sinT_ref[...]
    q_zero = jnp.zeros((QK_PAD - QK_DIM, tm), BF16)
    k_zero = jnp.zeros((tm, QK_PAD - QK_DIM), BF16)
    r1 = QK_NOPE_DIM + ROPE_HALF
    for h in range(n_heads):
        b = h * QK_PAD
        x1 = qT[b + QK_NOPE_DIM:b + r1]
        x2 = qT[b + r1:b + QK_DIM]
        qT_ref[0, h, 0:QK_NOPE_DIM, :] = qT[b:b + QK_NOPE_DIM].astype(BF16)
        qT_ref[0, h, QK_NOPE_DIM:r1, :] = (x1 * c - x2 * s).astype(BF16)
        qT_ref[0, h, r1:QK_DIM, :] = (x2 * c + x1 * s).astype(BF16)
        qT_ref[0, h, QK_DIM:QK_PAD, :] = q_zero
        k_ref[0, h, :, 0:QK_NOPE_DIM] = k_nope[:, h * QK_NOPE_DIM:(h + 1) * QK_NOPE_DIM].astype(BF16)
        k_ref[0, h, :, QK_NOPE_DIM:QK_DIM] = kr_rot
        k_ref[0, h, :, QK_DIM:QK_PAD] = k_zero
        vT_ref[0, h] = vT[h * V_DIM:(h + 1) * V_DIM].astype(BF16)


def _attn_prep(xn, w, rope, *, tm_pref=256):
    bsz, s, d = xn.shape
    n_heads = w["n_heads"]
    q_rank, kv_rank = w["q_rank"], w["kv_rank"]
    tm = _tile(s, tm_pref)
    cosT, sinT, cosK, sinK = rope
    kern = functools.partial(_attn_prep_kernel, n_heads=n_heads, q_rank=q_rank,
                             kv_rank=kv_rank, scale=QK_DIM ** -0.5)
    return pl.pallas_call(
        kern,
        out_shape=(jax.ShapeDtypeStruct((bsz, n_heads, QK_PAD, s), BF16),
                   jax.ShapeDtypeStruct((bsz, n_heads, s, QK_PAD), BF16),
                   jax.ShapeDtypeStruct((bsz, n_heads, V_DIM, s), BF16)),
        grid=(bsz, s // tm),
        in_specs=[pl.BlockSpec((1, tm, d), lambda b, i: (b, i, 0)),
                  _resident(w["w_small"].shape),
                  _resident((1, q_rank)),
                  _resident((1, kv_rank)),
                  _resident(w["w_uqT"].shape),
                  _resident(w["w_uk"].shape),
                  _resident(w["w_uvT"].shape),
                  pl.BlockSpec((ROPE_HALF, tm), lambda b, i: (0, i)),
                  pl.BlockSpec((ROPE_HALF, tm), lambda b, i: (0, i)),
                  pl.BlockSpec((tm, QK_ROPE_DIM), lambda b, i: (i, 0)),
                  pl.BlockSpec((tm, QK_ROPE_DIM), lambda b, i: (i, 0))],
        out_specs=(pl.BlockSpec((1, n_heads, QK_PAD, tm), lambda b, i: (b, 0, 0, i)),
                   pl.BlockSpec((1, n_heads, tm, QK_PAD), lambda b, i: (b, 0, i, 0)),
                   pl.BlockSpec((1, n_heads, V_DIM, tm), lambda b, i: (b, 0, 0, i))),
        compiler_params=_params("parallel", "parallel"),
        name="attn_prep",
    )(xn, w["w_small"], w["q_norm"], w["kv_norm"], w["w_uqT"], w["w_uk"], w["w_uvT"],
      cosT, sinT, cosK, sinK)


def _rope_tables(start, length):
    inv_freq = 1.0 / (ROPE_THETA ** (jnp.arange(0, QK_ROPE_DIM, 2, dtype=F32) / QK_ROPE_DIM))
    pos = jnp.arange(start, start + length, dtype=F32)
    ang = pos[:, None] * inv_freq[None, :]
    cos, sin = jnp.cos(ang), jnp.sin(ang)
    cosK = jnp.concatenate([cos, cos], axis=1)
    sinK = jnp.concatenate([-sin, sin], axis=1)
    return cos.T, sin.T, cosK, sinK


def _proj_kernel(*refs, mode):
    xn_ref, w1_ref = refs[0], refs[1]
    o_ref = refs[-1]
    xn = xn_ref[...]
    p1 = jnp.dot(xn, w1_ref[...], preferred_element_type=F32)
    if mode == "silu":
        out = jax.nn.silu(p1)
    elif mode == "sigmoid_bias":
        out = jax.nn.sigmoid(p1 + refs[2][...])
    else:
        p2 = jnp.dot(xn, refs[2][...], preferred_element_type=F32)
        out = p1 * p2 if mode == "mul" else p1 * jax.nn.silu(p2)
    o_ref[...] = out.astype(o_ref.dtype)


def _proj(xn, w1, second, mode, *, tm_pref=1024, tn_pref=1024):
    t, d = xn.shape
    n = w1.shape[1]
    tm, tn = _tile(t, tm_pref), _tile(n, tn_pref)
    w_spec = pl.BlockSpec((d, tn), lambda i, j: (0, j))
    in_specs = [pl.BlockSpec((tm, d), lambda i, j: (i, 0)), w_spec]
    args = [xn, w1]
    if mode == "sigmoid_bias":
        in_specs.append(pl.BlockSpec((1, tn), lambda i, j: (0, j)))
        args.append(second.reshape(1, n))
    elif mode != "silu":
        in_specs.append(w_spec)
        args.append(second)
    return pl.pallas_call(
        functools.partial(_proj_kernel, mode=mode),
        out_shape=jax.ShapeDtypeStruct((t, n), BF16),
        grid=(t // tm, n // tn),
        in_specs=in_specs,
        out_specs=pl.BlockSpec((tm, tn), lambda i, j: (i, j)),
        compiler_params=_params("parallel", "parallel"),
        name="proj_" + mode,
    )(*args)


def _flash_kernel(qT_ref, k_ref, vT_ref, km_ref, vTm_ref, sz_ref, o_ref, acc_ref,
                  *, tk, n_meta):
    qT = qT_ref[0, 0]
    tq = qT.shape[1]
    n_kv = k_ref.shape[2] // tk

    s = jnp.dot(km_ref[0], qT, preferred_element_type=F32)
    row = lax.broadcasted_iota(jnp.int32, s.shape, 0)
    s = jnp.where(row < n_meta, s, MASK_VALUE)
    m0 = jnp.max(s, axis=0, keepdims=True)
    p = jnp.exp(s - m0)
    l0 = jnp.sum(p, axis=0, keepdims=True)
    acc_ref[...] = jnp.dot(vTm_ref[0], p.astype(BF16), preferred_element_type=F32)

    def body(j, carry):
        m, l = carry
        off = pl.multiple_of(j * tk, tk)
        s = jnp.dot(k_ref[0, 0, pl.ds(off, tk), :], qT, preferred_element_type=F32)
        m_new = jnp.maximum(m, jnp.max(s, axis=0, keepdims=True))
        alpha = jnp.exp(m - m_new)
        p = jnp.exp(s - m_new)
        l = alpha * l + jnp.sum(p, axis=0, keepdims=True)
        pv = jnp.dot(vT_ref[0, 0, :, pl.ds(off, tk)], p.astype(BF16),
                     preferred_element_type=F32)
        acc_ref[...] = acc_ref[...] * alpha + pv
        return m_new, l

    _, l = lax.fori_loop(0, n_kv, body, (m0, l0))
    o = (acc_ref[...] / l).T
    o_ref[0] = (o * sz_ref[0].astype(F32)).astype(o_ref.dtype)


def _flash(qT, k, vT, k_meta, vT_meta, sz, *, n_meta, tq_pref=512, tk_pref=512):
    bsz, n_heads, _, s = qT.shape
    tq, tk = _tile(s, tq_pref), _tile(s, tk_pref)
    return pl.pallas_call(
        functools.partial(_flash_kernel, tk=tk, n_meta=n_meta),
        out_shape=jax.ShapeDtypeStruct((bsz, s, n_heads * V_DIM), BF16),
        grid=(bsz, n_heads, s // tq),
        in_specs=[pl.BlockSpec((1, 1, QK_PAD, tq), lambda b, h, i: (b, h, 0, i)),
                  pl.BlockSpec((1, 1, s, QK_PAD), lambda b, h, i: (b, h, 0, 0)),
                  pl.BlockSpec((1, 1, V_DIM, s), lambda b, h, i: (b, h, 0, 0)),
                  pl.BlockSpec((1, META_PAD, QK_PAD), lambda b, h, i: (h, 0, 0)),
                  pl.BlockSpec((1, V_DIM, META_PAD), lambda b, h, i: (h, 0, 0)),
                  pl.BlockSpec((1, tq, V_DIM), lambda b, h, i: (b, i, h))],
        out_specs=pl.BlockSpec((1, tq, V_DIM), lambda b, h, i: (b, i, h)),
        scratch_shapes=[pltpu.VMEM((V_DIM, tq), F32)],
        compiler_params=_params("parallel", "parallel", "arbitrary"),
        name="flash",
    )(qT, k, vT, k_meta, vT_meta, sz)


def _out_kernel(x_ref, oz_ref, u_ref, up_ref, un_ref, um_ref, bz_ref, ga_ref, gc_ref,
                cw_ref, wa_ref, wc_ref, wo_ref, fw_ref, y_ref, *, tiles_per_seq, n_meta):
    i = pl.program_id(0)
    tm = u_ref.shape[0]
    pos = i % tiles_per_seq
    u = u_ref[...].astype(F32)
    halo = up_ref.shape[0]
    prev_row = jnp.where(pos == 0,
                         um_ref[...].astype(F32)[n_meta - 1:n_meta],
                         up_ref[...].astype(F32)[halo - 1:halo])
    next_row = jnp.where(pos == tiles_per_seq - 1,
                         jnp.zeros((1, u.shape[1]), F32),
                         un_ref[...].astype(F32)[0:1])
    rid = lax.broadcasted_iota(jnp.int32, u.shape, 0)
    u_prev = jnp.where(rid == 0, prev_row, pltpu.roll(u, 1, 0))
    u_next = jnp.where(rid == tm - 1, next_row, pltpu.roll(u, tm - 1, 0))
    cw = cw_ref[...]
    conv = cw[0:1] * u_prev + cw[1:2] * u + cw[2:3] * u_next
    yc_in = (bz_ref[...].astype(F32) * conv).astype(BF16)
    y_conv = jnp.dot(yc_in, wc_ref[...], preferred_element_type=F32)
    y_attn = jnp.dot(oz_ref[...], wa_ref[...], preferred_element_type=F32)
    merged = ga_ref[...].astype(F32) * y_attn + gc_ref[...].astype(F32) * y_conv
    out = x_ref[...] + jnp.dot(merged.astype(BF16), wo_ref[...], preferred_element_type=F32)
    y_ref[...] = _rms(out, fw_ref[...])


def _out(x, oz, u, u_meta, bz, g, w, *, seq_len, n_meta, tm_pref=256):
    t, d = x.shape
    cw = u.shape[1]
    aw = oz.shape[1]
    tm = _tile(seq_len, tm_pref)
    halo = 16
    hb = tm // halo
    n_halo = t // halo
    row = lambda i: (i, 0)
    kern = functools.partial(_out_kernel, tiles_per_seq=seq_len // tm, n_meta=n_meta)
    return pl.pallas_call(
        kern,
        out_shape=jax.ShapeDtypeStruct((t, d), F32),
        grid=(t // tm,),
        in_specs=[pl.BlockSpec((tm, d), row),
                  pl.BlockSpec((tm, aw), row),
                  pl.BlockSpec((tm, cw), row),
                  pl.BlockSpec((halo, cw), lambda i: (jnp.maximum(i * hb - 1, 0), 0)),
                  pl.BlockSpec((halo, cw), lambda i: (jnp.minimum((i + 1) * hb, n_halo - 1), 0)),
                  pl.BlockSpec((halo, cw), lambda i: (0, 0)),
                  pl.BlockSpec((tm, cw), row),
                  pl.BlockSpec((tm, d), lambda i: (i, 0)),
                  pl.BlockSpec((tm, d), lambda i: (i, 1)),
                  _resident(w["conv_w"].shape),
                  _resident(w["w_o_attn"].shape),
                  _resident(w["w_o_conv"].shape),
                  _resident(w["w_o"].shape),
                  _resident((1, d))],
        out_specs=pl.BlockSpec((tm, d), row),
        compiler_params=_params("parallel"),
        name="out",
    )(x, oz, u, u, u, u_meta, bz, g, g, w["conv_w"], w["w_o_attn"], w["w_o_conv"], w["w_o"],
      w["final_norm"])


def _prep_weights(norm_w, w_in, b_gate, q_a_norm_w, w_uq, kv_a_norm_w, w_ukv,
                  w_o_attn, conv_w, w_o_conv, w_o, final_norm_w):
    d = w_in.shape[1]
    q_rank = q_a_norm_w.shape[-1]
    kv_rank = kv_a_norm_w.shape[-1]
    n_heads = w_uq.shape[-1] // QK_DIM
    aw = w_o_attn.shape[1]
    cwid = w_o_conv.shape[1]
    sizes = (q_rank, kv_rank, QK_ROPE_DIM, aw, cwid, cwid, cwid, cwid, 2 * d)
    assert sum(sizes) == w_in.shape[-1]
    cols, start = [], 0
    for sz in sizes:
        cols.append((start, start + sz))
        start += sz
    win = w_in[0]
    cut = lambda k: win[:, cols[k][0]:cols[k][1]].astype(BF16)
    wuq = w_uq[0].reshape(q_rank, n_heads, QK_DIM).transpose(1, 2, 0)
    wuq = jnp.pad(wuq, ((0, 0), (0, QK_PAD - QK_DIM), (0, 0)))
    wukv = w_ukv[0].reshape(kv_rank, n_heads, QK_NOPE_DIM + V_DIM)
    return dict(
        n_heads=n_heads, q_rank=q_rank, kv_rank=kv_rank,
        norm=norm_w[0],
        w_small=win[:, :cols[2][1]].astype(BF16),
        w_z=cut(3), w_cx=cut(4), w_cb=cut(5), w_cc=cut(6), w_zc=cut(7), w_g=cut(8),
        b_gate=b_gate[0],
        q_norm=q_a_norm_w[0].reshape(1, q_rank),
        kv_norm=kv_a_norm_w[0].reshape(1, kv_rank),
        w_uqT=wuq.reshape(n_heads * QK_PAD, q_rank).astype(BF16),
        w_uk=wukv[:, :, :QK_NOPE_DIM].reshape(kv_rank, n_heads * QK_NOPE_DIM).astype(BF16),
        w_uvT=wukv[:, :, QK_NOPE_DIM:].transpose(1, 2, 0).reshape(n_heads * V_DIM, kv_rank).astype(BF16),
        w_o_attn=w_o_attn[0].astype(BF16),
        conv_w=conv_w[0],
        w_o_conv=w_o_conv[0].astype(BF16),
        w_o=w_o[0].astype(BF16),
        final_norm=final_norm_w.reshape(1, d),
    )


def _meta_state(meta_tokens, w):
    n_meta, d = meta_tokens.shape
    xm = jnp.pad(meta_tokens, ((0, META_PAD - n_meta), (0, 0)))
    xn = _rmsnorm(xm, w["norm"])
    _, k_meta, vT_meta = _attn_prep(xn[None], w, _rope_tables(0, META_PAD))
    u_meta = _proj(xn, w["w_cx"], w["w_cc"], "mul")
    return k_meta[0], vT_meta[0], u_meta


def _trunk(x, meta, w, n_meta):
    bsz, s, d = x.shape
    k_meta, vT_meta, u_meta = meta
    x2 = x.reshape(bsz * s, d)
    xn = _rmsnorm(x2, w["norm"])
    qT, k, vT = _attn_prep(xn.reshape(bsz, s, d), w, _rope_tables(n_meta, s))
    sz = _proj(xn, w["w_z"], None, "silu")
    u = _proj(xn, w["w_cx"], w["w_cc"], "mul")
    bz = _proj(xn, w["w_cb"], w["w_zc"], "mul_silu")
    g = _proj(xn, w["w_g"], w["b_gate"], "sigmoid_bias")
    oz = _flash(qT, k, vT, k_meta, vT_meta, sz.reshape(bsz, s, -1), n_meta=n_meta)
    y = _out(x2, oz.reshape(bsz * s, -1), u, u_meta, bz, g, w, seq_len=s, n_meta=n_meta)
    return y.reshape(bsz, s, d)


def kernel(x_prompt, x_sample, meta_tokens, norm_w, w_in, b_gate, q_a_norm_w, w_uq, kv_a_norm_w, w_ukv, w_o_attn, conv_w, w_o_conv, w_o, final_norm_w):
    assert norm_w.shape[0] == 1
    n_meta = meta_tokens.shape[0]
    assert n_meta <= 16
    w = _prep_weights(norm_w, w_in, b_gate, q_a_norm_w, w_uq, kv_a_norm_w, w_ukv,
                      w_o_attn, conv_w, w_o_conv, w_o, final_norm_w)
    meta = _meta_state(meta_tokens, w)
    return (_trunk(x_prompt, meta, w, n_meta), _trunk(x_sample, meta, w, n_meta))
```

```python
import functools

import jax
import jax.numpy as jnp
from jax import lax
from jax.experimental import pallas as pl
from jax.experimental.pallas import tpu as pltpu

QK_NOPE_DIM = 128
QK_ROPE_DIM = 64
QK_DIM = QK_NOPE_DIM + QK_ROPE_DIM
V_DIM = 128
ROPE_HALF = QK_ROPE_DIM // 2
QK_PAD = 256
ROPE_THETA = 10000.0
NORM_EPS = 1e-6
MASK_VALUE = -1e30
LOG2_E = 1.4426950408889634
META_PAD = 128
VMEM_LIMIT_BYTES = 56 * 1024 * 1024

F32 = jnp.float32
BF16 = jnp.bfloat16


def _params(*semantics):
    return pltpu.CompilerParams(dimension_semantics=semantics,
                                vmem_limit_bytes=VMEM_LIMIT_BYTES)


def _tile(n, pref):
    if n <= pref:
        return n
    t = pref
    while n % t:
        t //= 2
    return t


def _nt_dot(a, b):
    return lax.dot_general(a, b, (((1,), (1,)), ((), ())), preferred_element_type=F32)


def _rms(x, w):
    return x * lax.rsqrt(jnp.mean(x * x, axis=-1, keepdims=True) + NORM_EPS) * w


def _resident(shape):
    nd = len(shape)
    return pl.BlockSpec(shape, lambda *_: (0,) * nd, pipeline_mode=pl.Buffered(1))


def _rmsnorm_kernel(x_ref, w_ref, o_ref):
    o_ref[...] = _rms(x_ref[...], w_ref[...]).astype(o_ref.dtype)


def _rmsnorm(x, w):
    t, d = x.shape
    tm = _tile(t, 512)
    return pl.pallas_call(
        _rmsnorm_kernel,
        out_shape=jax.ShapeDtypeStruct((t, d), BF16),
        grid=(t // tm,),
        in_specs=[pl.BlockSpec((tm, d), lambda i: (i, 0)),
                  pl.BlockSpec((1, d), lambda i: (0, 0))],
        out_specs=pl.BlockSpec((tm, d), lambda i: (i, 0)),
        compiler_params=_params("parallel"),
        name="rmsnorm",
    )(x, w.reshape(1, d))


def _attn_prep_kernel(xn_ref, wsm_ref, qnw_ref, kvnw_ref, wuqT_ref, wuk_ref, wuvT_ref,
                      cosT_ref, sinT_ref, cosK_ref, sinK_ref,
                      qT_ref, k_ref, vT_ref, *, n_heads, q_rank, kv_rank, scale):
    tm = xn_ref.shape[1]
    p = jnp.dot(xn_ref[0], wsm_ref[...], preferred_element_type=F32)
    qn = _rms(p[:, :q_rank], qnw_ref[...]).astype(BF16)
    cn = _rms(p[:, q_rank:q_rank + kv_rank], kvnw_ref[...]).astype(BF16)
    kr = p[:, q_rank + kv_rank:]
    kr_sw = jnp.concatenate([kr[:, ROPE_HALF:], kr[:, :ROPE_HALF]], axis=1)
    kr_rot = (kr * cosK_ref[...] + kr_sw * sinK_ref[...]).astype(BF16)

    qT = _nt_dot(wuqT_ref[...], qn) * scale
    k_nope = jnp.dot(cn, wuk_ref[...], preferred_element_type=F32)
    vT = _nt_dot(wuvT_ref[...], cn)
    c = cosT_ref[...]
    s = sinT_ref[...]
    q_zero = jnp.zeros((QK_PAD - QK_DIM, tm), BF16)
    k_zero = jnp.zeros((tm, QK_PAD - QK_DIM), BF16)
    r1 = QK_NOPE_DIM + ROPE_HALF
    for h in range(n_heads):
        b = h * QK_PAD
        x1 = qT[b + QK_NOPE_DIM:b + r1]
        x2 = qT[b + r1:b + QK_DIM]
        qT_ref[0, h, 0:QK_NOPE_DIM, :] = qT[b:b + QK_NOPE_DIM].astype(BF16)
        qT_ref[0, h, QK_NOPE_DIM:r1, :] = (x1 * c - x2 * s).astype(BF16)
        qT_ref[0, h, r1:QK_DIM, :] = (x2 * c + x1 * s).astype(BF16)
        qT_ref[0, h, QK_DIM:QK_PAD, :] = q_zero
        k_ref[0, h, :, 0:QK_NOPE_DIM] = k_nope[:, h * QK_NOPE_DIM:(h + 1) * QK_NOPE_DIM].astype(BF16)
        k_ref[0, h, :, QK_NOPE_DIM:QK_DIM] = kr_rot
        k_ref[0, h, :, QK_DIM:QK_PAD] = k_zero
        vT_ref[0, h] = vT[h * V_DIM:(h + 1) * V_DIM].astype(BF16)


def _attn_prep(xn, w, rope, *, tm_pref=256):
    bsz, s, d = xn.shape
    n_heads = w["n_heads"]
    q_rank, kv_rank = w["q_rank"], w["kv_rank"]
    tm = _tile(s, tm_pref)
    cosT, sinT, cosK, sinK = rope
    kern = functools.partial(_attn_prep_kernel, n_heads=n_heads, q_rank=q_rank,
                             kv_rank=kv_rank, scale=QK_DIM ** -0.5 * LOG2_E)
    return pl.pallas_call(
        kern,
        out_shape=(jax.ShapeDtypeStruct((bsz, n_heads, QK_PAD, s), BF16),
                   jax.ShapeDtypeStruct((bsz, n_heads, s, QK_PAD), BF16),
                   jax.ShapeDtypeStruct((bsz, n_heads, V_DIM, s), BF16)),
        grid=(bsz, s // tm),
        in_specs=[pl.BlockSpec((1, tm, d), lambda b, i: (b, i, 0)),
                  _resident(w["w_small"].shape),
                  _resident((1, q_rank)),
                  _resident((1, kv_rank)),
                  _resident(w["w_uqT"].shape),
                  _resident(w["w_uk"].shape),
                  _resident(w["w_uvT"].shape),
                  pl.BlockSpec((ROPE_HALF, tm), lambda b, i: (0, i)),
                  pl.BlockSpec((ROPE_HALF, tm), lambda b, i: (0, i)),
                  pl.BlockSpec((tm, QK_ROPE_DIM), lambda b, i: (i, 0)),
                  pl.BlockSpec((tm, QK_ROPE_DIM), lambda b, i: (i, 0))],
        out_specs=(pl.BlockSpec((1, n_heads, QK_PAD, tm), lambda b, i: (b, 0, 0, i)),
                   pl.BlockSpec((1, n_heads, tm, QK_PAD), lambda b, i: (b, 0, i, 0)),
                   pl.BlockSpec((1, n_heads, V_DIM, tm), lambda b, i: (b, 0, 0, i))),
        compiler_params=_params("parallel", "parallel"),
        name="attn_prep",
    )(xn, w["w_small"], w["q_norm"], w["kv_norm"], w["w_uqT"], w["w_uk"], w["w_uvT"],
      cosT, sinT, cosK, sinK)


def _rope_tables(start, length):
    inv_freq = 1.0 / (ROPE_THETA ** (jnp.arange(0, QK_ROPE_DIM, 2, dtype=F32) / QK_ROPE_DIM))
    pos = jnp.arange(start, start + length, dtype=F32)
    ang = pos[:, None] * inv_freq[None, :]
    cos, sin = jnp.cos(ang), jnp.sin(ang)
    cosK = jnp.concatenate([cos, cos], axis=1)
    sinK = jnp.concatenate([-sin, sin], axis=1)
    return cos.T, sin.T, cosK, sinK


def _proj_kernel(*refs, mode):
    xn_ref, w1_ref = refs[0], refs[1]
    o_ref = refs[-1]
    xn = xn_ref[...]
    p1 = jnp.dot(xn, w1_ref[...], preferred_element_type=F32)
    if mode == "silu":
        out = jax.nn.silu(p1)
    elif mode == "sigmoid_bias":
        out = jax.nn.sigmoid(p1 + refs[2][...])
    else:
        p2 = jnp.dot(xn, refs[2][...], preferred_element_type=F32)
        out = p1 * p2 if mode == "mul" else p1 * jax.nn.silu(p2)
    o_ref[...] = out.astype(o_ref.dtype)


def _proj(xn, w1, second, mode, *, tm_pref=1024, tn_pref=1024):
    t, d = xn.shape
    n = w1.shape[1]
    tm, tn = _tile(t, tm_pref), _tile(n, tn_pref)
    w_spec = pl.BlockSpec((d, tn), lambda i, j: (0, j))
    in_specs = [pl.BlockSpec((tm, d), lambda i, j: (i, 0)), w_spec]
    args = [xn, w1]
    if mode == "sigmoid_bias":
        in_specs.append(pl.BlockSpec((1, tn), lambda i, j: (0, j)))
        args.append(second.reshape(1, n))
    elif mode != "silu":
        in_specs.append(w_spec)
        args.append(second)
    return pl.pallas_call(
        functools.partial(_proj_kernel, mode=mode),
        out_shape=jax.ShapeDtypeStruct((t, n), BF16),
        grid=(t // tm, n // tn),
        in_specs=in_specs,
        out_specs=pl.BlockSpec((tm, tn), lambda i, j: (i, j)),
        compiler_params=_params("parallel", "parallel"),
        name="proj_" + mode,
    )(*args)


def _flash_kernel(qT_ref, k_ref, vT_ref, km_ref, vTm_ref, sz_ref, o_ref, acc_ref, s_ref,
                  *, tk, n_meta):
    qT = qT_ref[0, 0]
    n_pairs = k_ref.shape[2] // (2 * tk)

    def scores(chunk, slot):
        off = pl.multiple_of(chunk * tk, tk)
        s_ref[slot] = jnp.dot(k_ref[0, 0, pl.ds(off, tk), :], qT, preferred_element_type=F32)

    def consume(chunk, slot, m, l):
        off = pl.multiple_of(chunk * tk, tk)
        s = s_ref[slot]
        m_new = jnp.maximum(m, jnp.max(s, axis=0, keepdims=True))
        alpha = jnp.exp2(m - m_new)
        p = jnp.exp2(s - m_new)
        l = alpha * l + jnp.sum(p, axis=0, keepdims=True)
        pv = jnp.dot(vT_ref[0, 0, :, pl.ds(off, tk)], p.astype(BF16),
                     preferred_element_type=F32)
        acc_ref[...] = acc_ref[...] * alpha + pv
        return m_new, l

    scores(0, 0)
    s = jnp.dot(km_ref[0], qT, preferred_element_type=F32)
    row = lax.broadcasted_iota(jnp.int32, s.shape, 0)
    s = jnp.where(row < n_meta, s, MASK_VALUE)
    m = jnp.max(s, axis=0, keepdims=True)
    p = jnp.exp2(s - m)
    l = jnp.sum(p, axis=0, keepdims=True)
    acc_ref[...] = jnp.dot(vTm_ref[0], p.astype(BF16), preferred_element_type=F32)

    def pair(jj, carry):
        m, l = carry
        a = 2 * jj
        scores(a + 1, 1)
        m, l = consume(a, 0, m, l)
        scores(a + 2, 0)
        return consume(a + 1, 1, m, l)

    m, l = lax.fori_loop(0, n_pairs - 1, pair, (m, l))
    a = 2 * (n_pairs - 1)
    scores(a + 1, 1)
    m, l = consume(a, 0, m, l)
    m, l = consume(a + 1, 1, m, l)
    o = (acc_ref[...] / l).T
    o_ref[0] = (o * sz_ref[0].astype(F32)).astype(o_ref.dtype)


def _flash(qT, k, vT, k_meta, vT_meta, sz, *, n_meta, tq_pref=512, tk_pref=1024):
    bsz, n_heads, _, s = qT.shape
    tq = _tile(s, tq_pref)
    tk = _tile(s // 2, tk_pref)
    return pl.pallas_call(
        functools.partial(_flash_kernel, tk=tk, n_meta=n_meta),
        out_shape=jax.ShapeDtypeStruct((bsz, s, n_heads * V_DIM), BF16),
        grid=(bsz, n_heads, s // tq),
        in_specs=[pl.BlockSpec((1, 1, QK_PAD, tq), lambda b, h, i: (b, h, 0, i)),
                  pl.BlockSpec((1, 1, s, QK_PAD), lambda b, h, i: (b, h, 0, 0)),
                  pl.BlockSpec((1, 1, V_DIM, s), lambda b, h, i: (b, h, 0, 0)),
                  pl.BlockSpec((1, META_PAD, QK_PAD), lambda b, h, i: (h, 0, 0)),
                  pl.BlockSpec((1, V_DIM, META_PAD), lambda b, h, i: (h, 0, 0)),
                  pl.BlockSpec((1, tq, V_DIM), lambda b, h, i: (b, i, h))],
        out_specs=pl.BlockSpec((1, tq, V_DIM), lambda b, h, i: (b, i, h)),
        scratch_shapes=[pltpu.VMEM((V_DIM, tq), F32), pltpu.VMEM((2, tk, tq), F32)],
        compiler_params=_params("parallel", "parallel", "arbitrary"),
        name="flash",
    )(qT, k, vT, k_meta, vT_meta, sz)


def _out_kernel(x_ref, oz_ref, u_ref, up_ref, un_ref, um_ref, bz_ref, ga_ref, gc_ref,
                cw_ref, wa_ref, wc_ref, wo_ref, fw_ref, y_ref, *, tiles_per_seq, n_meta):
    i = pl.program_id(0)
    tm = u_ref.shape[0]
    pos = i % tiles_per_seq
    u = u_ref[...].astype(F32)
    halo = up_ref.shape[0]
    prev_row = jnp.where(pos == 0,
                         um_ref[...].astype(F32)[n_meta - 1:n_meta],
                         up_ref[...].astype(F32)[halo - 1:halo])
    next_row = jnp.where(pos == tiles_per_seq - 1,
                         jnp.zeros((1, u.shape[1]), F32),
                         un_ref[...].astype(F32)[0:1])
    rid = lax.broadcasted_iota(jnp.int32, u.shape, 0)
    u_prev = jnp.where(rid == 0, prev_row, pltpu.roll(u, 1, 0))
    u_next = jnp.where(rid == tm - 1, next_row, pltpu.roll(u, tm - 1, 0))
    cw = cw_ref[...]
    conv = cw[0:1] * u_prev + cw[1:2] * u + cw[2:3] * u_next
    yc_in = (bz_ref[...].astype(F32) * conv).astype(BF16)
    y_conv = jnp.dot(yc_in, wc_ref[...], preferred_element_type=F32)
    y_attn = jnp.dot(oz_ref[...], wa_ref[...], preferred_element_type=F32)
    merged = ga_ref[...].astype(F32) * y_attn + gc_ref[...].astype(F32) * y_conv
    out = x_ref[...] + jnp.dot(merged.astype(BF16), wo_ref[...], preferred_element_type=F32)
    y_ref[...] = _rms(out, fw_ref[...])


def _out(x, oz, u, u_meta, bz, g, w, *, seq_len, n_meta, tm_pref=256):
    t, d = x.shape
    cw = u.shape[1]
    aw = oz.shape[1]
    tm = _tile(seq_len, tm_pref)
    halo = 16
    hb = tm // halo
    n_halo = t // halo
    row = lambda i: (i, 0)
    kern = functools.partial(_out_kernel, tiles_per_seq=seq_len // tm, n_meta=n_meta)
    return pl.pallas_call(
        kern,
        out_shape=jax.ShapeDtypeStruct((t, d), F32),
        grid=(t // tm,),
        in_specs=[pl.BlockSpec((tm, d), row),
                  pl.BlockSpec((tm, aw), row),
                  pl.BlockSpec((tm, cw), row),
                  pl.BlockSpec((halo, cw), lambda i: (jnp.maximum(i * hb - 1, 0), 0)),
                  pl.BlockSpec((halo, cw), lambda i: (jnp.minimum((i + 1) * hb, n_halo - 1), 0)),
                  pl.BlockSpec((halo, cw), lambda i: (0, 0)),
                  pl.BlockSpec((tm, cw), row),
                  pl.BlockSpec((tm, d), lambda i: (i, 0)),
                  pl.BlockSpec((tm, d), lambda i: (i, 1)),
                  _resident(w["conv_w"].shape),
                  _resident(w["w_o_attn"].shape),
                  _resident(w["w_o_conv"].shape),
                  _resident(w["w_o"].shape),
                  _resident((1, d))],
        out_specs=pl.BlockSpec((tm, d), row),
        compiler_params=_params("parallel"),
        name="out",
    )(x, oz, u, u, u, u_meta, bz, g, g, w["conv_w"], w["w_o_attn"], w["w_o_conv"], w["w_o"],
      w["final_norm"])


def _prep_weights(norm_w, w_in, b_gate, q_a_norm_w, w_uq, kv_a_norm_w, w_ukv,
                  w_o_attn, conv_w, w_o_conv, w_o, final_norm_w):
    d = w_in.shape[1]
    q_rank = q_a_norm_w.shape[-1]
    kv_rank = kv_a_norm_w.shape[-1]
    n_heads = w_uq.shape[-1] // QK_DIM
    aw = w_o_attn.shape[1]
    cwid = w_o_conv.shape[1]
    sizes = (q_rank, kv_rank, QK_ROPE_DIM, aw, cwid, cwid, cwid, cwid, 2 * d)
    assert sum(sizes) == w_in.shape[-1]
    cols, start = [], 0
    for sz in sizes:
        cols.append((start, start + sz))
        start += sz
    win = w_in[0]
    cut = lambda k: win[:, cols[k][0]:cols[k][1]].astype(BF16)
    wuq = w_uq[0].reshape(q_rank, n_heads, QK_DIM).transpose(1, 2, 0)
    wuq = jnp.pad(wuq, ((0, 0), (0, QK_PAD - QK_DIM), (0, 0)))
    wukv = w_ukv[0].reshape(kv_rank, n_heads, QK_NOPE_DIM + V_DIM)
    return dict(
        n_heads=n_heads, q_rank=q_rank, kv_rank=kv_rank,
        norm=norm_w[0],
        w_small=win[:, :cols[2][1]].astype(BF16),
        w_z=cut(3), w_cx=cut(4), w_cb=cut(5), w_cc=cut(6), w_zc=cut(7), w_g=cut(8),
        b_gate=b_gate[0],
        q_norm=q_a_norm_w[0].reshape(1, q_rank),
        kv_norm=kv_a_norm_w[0].reshape(1, kv_rank),
        w_uqT=wuq.reshape(n_heads * QK_PAD, q_rank).astype(BF16),
        w_uk=wukv[:, :, :QK_NOPE_DIM].reshape(kv_rank, n_heads * QK_NOPE_DIM).astype(BF16),
        w_uvT=wukv[:, :, QK_NOPE_DIM:].transpose(1, 2, 0).reshape(n_heads * V_DIM, kv_rank).astype(BF16),
        w_o_attn=w_o_attn[0].astype(BF16),
        conv_w=conv_w[0],
        w_o_conv=w_o_conv[0].astype(BF16),
        w_o=w_o[0].astype(BF16),
        final_norm=final_norm_w.reshape(1, d),
    )


def _meta_state(meta_tokens, w):
    n_meta, d = meta_tokens.shape
    xm = jnp.pad(meta_tokens, ((0, META_PAD - n_meta), (0, 0)))
    xn = _rmsnorm(xm, w["norm"])
    _, k_meta, vT_meta = _attn_prep(xn[None], w, _rope_tables(0, META_PAD))
    u_meta = _proj(xn, w["w_cx"], w["w_cc"], "mul")
    return k_meta[0], vT_meta[0], u_meta


def _trunk(x, meta, w, n_meta):
    bsz, s, d = x.shape
    k_meta, vT_meta, u_meta = meta
    x2 = x.reshape(bsz * s, d)
    xn = _rmsnorm(x2, w["norm"])
    qT, k, vT = _attn_prep(xn.reshape(bsz, s, d), w, _rope_tables(n_meta, s))
    sz = _proj(xn, w["w_z"], None, "silu")
    u = _proj(xn, w["w_cx"], w["w_cc"], "mul")
    bz = _proj(xn, w["w_cb"], w["w_zc"], "mul_silu")
    g = _proj(xn, w["w_g"], w["b_gate"], "sigmoid_bias")
    oz = _flash(qT, k, vT, k_meta, vT_meta, sz.reshape(bsz, s, -1), n_meta=n_meta)
    y = _out(x2, oz.reshape(bsz * s, -1), u, u_meta, bz, g, w, seq_len=s, n_meta=n_meta)
    return y.reshape(bsz, s, d)


def kernel(x_prompt, x_sample, meta_tokens, norm_w, w_in, b_gate, q_a_norm_w, w_uq, kv_a_norm_w, w_ukv, w_o_attn, conv_w, w_o_conv, w_o, final_norm_w):
    assert norm_w.shape[0] == 1
    n_meta = meta_tokens.shape[0]
    assert n_meta <= 16
    w = _prep_weights(norm_w, w_in, b_gate, q_a_norm_w, w_uq, kv_a_norm_w, w_ukv,
                      w_o_attn, conv_w, w_o_conv, w_o, final_norm_w)
    meta = _meta_state(meta_tokens, w)
    return (_trunk(x_prompt, meta, w, n_meta), _trunk(x_sample, meta, w, n_meta))
```

```python
import functools

import jax
import jax.numpy as jnp
from jax import lax
from jax.experimental import pallas as pl
from jax.experimental.pallas import tpu as pltpu

QK_NOPE_DIM = 128
QK_ROPE_DIM = 64
QK_DIM = QK_NOPE_DIM + QK_ROPE_DIM
V_DIM = 128
ROPE_HALF = QK_ROPE_DIM // 2
QK_PAD = 256
ROPE_THETA = 10000.0
NORM_EPS = 1e-6
MASK_VALUE = -1e30
LOG2_E = 1.4426950408889634
META_PAD = 128
VMEM_LIMIT_BYTES = 56 * 1024 * 1024

F32 = jnp.float32
BF16 = jnp.bfloat16


def _params(*semantics):
    return pltpu.CompilerParams(dimension_semantics=semantics,
                                vmem_limit_bytes=VMEM_LIMIT_BYTES)


def _tile(n, pref):
    if n <= pref:
        return n
    t = pref
    while n % t:
        t //= 2
    return t


def _nt_dot(a, b):
    return lax.dot_general(a, b, (((1,), (1,)), ((), ())), preferred_element_type=F32)


def _rms(x, w):
    return x * lax.rsqrt(jnp.mean(x * x, axis=-1, keepdims=True) + NORM_EPS) * w


def _resident(shape):
    nd = len(shape)
    return pl.BlockSpec(shape, lambda *_: (0,) * nd, pipeline_mode=pl.Buffered(1))


def _rmsnorm_kernel(x_ref, w_ref, o_ref):
    o_ref[...] = _rms(x_ref[...], w_ref[...]).astype(o_ref.dtype)


def _rmsnorm(x, w):
    t, d = x.shape
    tm = _tile(t, 512)
    return pl.pallas_call(
        _rmsnorm_kernel,
        out_shape=jax.ShapeDtypeStruct((t, d), BF16),
        grid=(t // tm,),
        in_specs=[pl.BlockSpec((tm, d), lambda i: (i, 0)),
                  pl.BlockSpec((1, d), lambda i: (0, 0))],
        out_specs=pl.BlockSpec((tm, d), lambda i: (i, 0)),
        compiler_params=_params("parallel"),
        name="rmsnorm",
    )(x, w.reshape(1, d))


def _attn_prep_kernel(xn_ref, wsm_ref, qnw_ref, kvnw_ref, wuqT_ref, wuk_ref, wuvT_ref,
                      cosT_ref, sinT_ref, cosK_ref, sinK_ref,
                      qT_ref, k_ref, vT_ref, *, n_heads, q_rank, kv_rank, scale):
    tm = xn_ref.shape[1]
    p = jnp.dot(xn_ref[0], wsm_ref[...], preferred_element_type=F32)
    qn = _rms(p[:, :q_rank], qnw_ref[...]).astype(BF16)
    cn = _rms(p[:, q_rank:q_rank + kv_rank], kvnw_ref[...]).astype(BF16)
    kr = p[:, q_rank + kv_rank:]
    kr_sw = jnp.concatenate([kr[:, ROPE_HALF:], kr[:, :ROPE_HALF]], axis=1)
    kr_rot = (kr * cosK_ref[...] + kr_sw * sinK_ref[...]).astype(BF16)

    qT = _nt_dot(wuqT_ref[...], qn) * scale
    k_nope = jnp.dot(cn, wuk_ref[...], preferred_element_type=F32)
    vT = _nt_dot(wuvT_ref[...], cn)
    c = cosT_ref[...]
    s = sinT_ref[...]
    q_zero = jnp.zeros((QK_PAD - QK_DIM, tm), BF16)
    k_zero = jnp.zeros((tm, QK_PAD - QK_DIM), BF16)
    r1 = QK_NOPE_DIM + ROPE_HALF
    for h in range(n_heads):
        b = h * QK_DIM
        x1 = qT[b + QK_NOPE_DIM:b + r1]
        x2 = qT[b + r1:b + QK_DIM]
        qT_ref[0, h, 0:QK_NOPE_DIM, :] = qT[b:b + QK_NOPE_DIM].astype(BF16)
        qT_ref[0, h, QK_NOPE_DIM:r1, :] = (x1 * c - x2 * s).astype(BF16)
        qT_ref[0, h, r1:QK_DIM, :] = (x2 * c + x1 * s).astype(BF16)
        qT_ref[0, h, QK_DIM:QK_PAD, :] = q_zero
        k_ref[0, h, :, 0:QK_NOPE_DIM] = k_nope[:, h * QK_NOPE_DIM:(h + 1) * QK_NOPE_DIM].astype(BF16)
        k_ref[0, h, :, QK_NOPE_DIM:QK_DIM] = kr_rot
        k_ref[0, h, :, QK_DIM:QK_PAD] = k_zero
        vT_ref[0, h] = vT[h * V_DIM:(h + 1) * V_DIM].astype(BF16)


def _attn_prep(xn, w, rope, *, tm_pref=256):
    bsz, s, d = xn.shape
    n_heads = w["n_heads"]
    q_rank, kv_rank = w["q_rank"], w["kv_rank"]
    tm = _tile(s, tm_pref)
    cosT, sinT, cosK, sinK = rope
    kern = functools.partial(_attn_prep_kernel, n_heads=n_heads, q_rank=q_rank,
                             kv_rank=kv_rank, scale=QK_DIM ** -0.5 * LOG2_E)
    return pl.pallas_call(
        kern,
        out_shape=(jax.ShapeDtypeStruct((bsz, n_heads, QK_PAD, s), BF16),
                   jax.ShapeDtypeStruct((bsz, n_heads, s, QK_PAD), BF16),
                   jax.ShapeDtypeStruct((bsz, n_heads, V_DIM, s), BF16)),
        grid=(bsz, s // tm),
        in_specs=[pl.BlockSpec((1, tm, d), lambda b, i: (b, i, 0)),
                  _resident(w["w_small"].shape),
                  _resident((1, q_rank)),
                  _resident((1, kv_rank)),
                  _resident(w["w_uqT"].shape),
                  _resident(w["w_uk"].shape),
                  _resident(w["w_uvT"].shape),
                  pl.BlockSpec((ROPE_HALF, tm), lambda b, i: (0, i)),
                  pl.BlockSpec((ROPE_HALF, tm), lambda b, i: (0, i)),
                  pl.BlockSpec((tm, QK_ROPE_DIM), lambda b, i: (i, 0)),
                  pl.BlockSpec((tm, QK_ROPE_DIM), lambda b, i: (i, 0))],
        out_specs=(pl.BlockSpec((1, n_heads, QK_PAD, tm), lambda b, i: (b, 0, 0, i)),
                   pl.BlockSpec((1, n_heads, tm, QK_PAD), lambda b, i: (b, 0, i, 0)),
                   pl.BlockSpec((1, n_heads, V_DIM, tm), lambda b, i: (b, 0, 0, i))),
        compiler_params=_params("parallel", "parallel"),
        name="attn_prep",
    )(xn, w["w_small"], w["q_norm"], w["kv_norm"], w["w_uqT"], w["w_uk"], w["w_uvT"],
      cosT, sinT, cosK, sinK)


def _rope_tables(start, length):
    inv_freq = 1.0 / (ROPE_THETA ** (jnp.arange(0, QK_ROPE_DIM, 2, dtype=F32) / QK_ROPE_DIM))
    pos = jnp.arange(start, start + length, dtype=F32)
    ang = pos[:, None] * inv_freq[None, :]
    cos, sin = jnp.cos(ang), jnp.sin(ang)
    cosK = jnp.concatenate([cos, cos], axis=1)
    sinK = jnp.concatenate([-sin, sin], axis=1)
    return cos.T, sin.T, cosK, sinK


def _proj_kernel(*refs, mode):
    xn_ref, w1_ref = refs[0], refs[1]
    o_ref = refs[-1]
    xn = xn_ref[...]
    p1 = jnp.dot(xn, w1_ref[...], preferred_element_type=F32)
    if mode == "silu":
        out = jax.nn.silu(p1)
    elif mode == "sigmoid_bias":
        out = jax.nn.sigmoid(p1 + refs[2][...])
    else:
        p2 = jnp.dot(xn, refs[2][...], preferred_element_type=F32)
        out = p1 * p2 if mode == "mul" else p1 * jax.nn.silu(p2)
    o_ref[...] = out.astype(o_ref.dtype)


def _proj(xn, w1, second, mode, *, tm_pref=1024, tn_pref=1024):
    t, d = xn.shape
    n = w1.shape[1]
    tm, tn = _tile(t, tm_pref), _tile(n, tn_pref)
    w_spec = pl.BlockSpec((d, tn), lambda i, j: (0, j))
    in_specs = [pl.BlockSpec((tm, d), lambda i, j: (i, 0)), w_spec]
    args = [xn, w1]
    if mode == "sigmoid_bias":
        in_specs.append(pl.BlockSpec((1, tn), lambda i, j: (0, j)))
        args.append(second.reshape(1, n))
    elif mode != "silu":
        in_specs.append(w_spec)
        args.append(second)
    return pl.pallas_call(
        functools.partial(_proj_kernel, mode=mode),
        out_shape=jax.ShapeDtypeStruct((t, n), BF16),
        grid=(t // tm, n // tn),
        in_specs=in_specs,
        out_specs=pl.BlockSpec((tm, tn), lambda i, j: (i, j)),
        compiler_params=_params("parallel", "parallel"),
        name="proj_" + mode,
    )(*args)


def _flash_kernel(qT_ref, k_ref, vT_ref, km_ref, vTm_ref, sz_ref, o_ref, acc_ref, s_ref, p_ref,
                  *, tk, n_meta):
    qT = qT_ref[0, 0]
    n_kv = k_ref.shape[2] // tk

    def chunk_start(c):
        return c * tk if isinstance(c, int) else pl.multiple_of(c * tk, tk)

    def stage(c, par, mx, m, l, alpha, *, do_scores=True, do_pv=True):
        mx_next = mx
        if do_scores:
            off = chunk_start(c + 1)
            s = jnp.dot(k_ref[0, 0, pl.ds(off, tk), :], qT, preferred_element_type=F32)
            s_ref[1 - par] = s
            mx_next = jnp.max(s, axis=0, keepdims=True)
        if do_pv:
            off = chunk_start(c - 1)
            pv = jnp.dot(vT_ref[0, 0, :, pl.ds(off, tk)], p_ref[1 - par],
                         preferred_element_type=F32)
            acc_ref[...] = acc_ref[...] * alpha + pv
        m_new = jnp.maximum(m, mx)
        alpha = jnp.exp2(m - m_new)
        p = jnp.exp2(s_ref[par] - m_new)
        p_ref[par] = p.astype(BF16)
        l = alpha * l + jnp.sum(p, axis=0, keepdims=True)
        return mx_next, m_new, l, alpha

    s = jnp.dot(k_ref[0, 0, pl.ds(0, tk), :], qT, preferred_element_type=F32)
    s_ref[0] = s
    mx = jnp.max(s, axis=0, keepdims=True)
    s = jnp.dot(km_ref[0], qT, preferred_element_type=F32)
    row = lax.broadcasted_iota(jnp.int32, s.shape, 0)
    s = jnp.where(row < n_meta, s, MASK_VALUE)
    m = jnp.max(s, axis=0, keepdims=True)
    p = jnp.exp2(s - m)
    l = jnp.sum(p, axis=0, keepdims=True)
    acc_ref[...] = jnp.dot(vTm_ref[0], p.astype(BF16), preferred_element_type=F32)

    carry = stage(0, 0, mx, m, l, None, do_pv=False)

    def pair(t, carry):
        c = 2 * t + 1
        carry = stage(c, 1, *carry)
        return stage(c + 1, 0, *carry)

    carry = lax.fori_loop(0, (n_kv - 2) // 2, pair, carry)
    _, m, l, alpha = stage(n_kv - 1, 1, *carry, do_scores=False)
    off = (n_kv - 1) * tk
    pv = jnp.dot(vT_ref[0, 0, :, pl.ds(off, tk)], p_ref[1], preferred_element_type=F32)
    o = ((acc_ref[...] * alpha + pv) / l).T
    o_ref[0] = (o * sz_ref[0].astype(F32)).astype(o_ref.dtype)


def _flash(qT, k, vT, k_meta, vT_meta, sz, *, n_meta, tq_pref=1024, tk_pref=1024):
    bsz, n_heads, _, s = qT.shape
    tq = _tile(s, tq_pref if s > 4096 else 2048)
    tk = _tile(s // 8, tk_pref)
    assert (s // tk) % 2 == 0
    return pl.pallas_call(
        functools.partial(_flash_kernel, tk=tk, n_meta=n_meta),
        out_shape=jax.ShapeDtypeStruct((bsz, s, n_heads * V_DIM), BF16),
        grid=(bsz, n_heads, s // tq),
        in_specs=[pl.BlockSpec((1, 1, QK_PAD, tq), lambda b, h, i: (b, h, 0, i)),
                  pl.BlockSpec((1, 1, s, QK_PAD), lambda b, h, i: (b, h, 0, 0)),
                  pl.BlockSpec((1, 1, V_DIM, s), lambda b, h, i: (b, h, 0, 0)),
                  pl.BlockSpec((1, META_PAD, QK_PAD), lambda b, h, i: (h, 0, 0)),
                  pl.BlockSpec((1, V_DIM, META_PAD), lambda b, h, i: (h, 0, 0)),
                  pl.BlockSpec((1, tq, V_DIM), lambda b, h, i: (b, i, h))],
        out_specs=pl.BlockSpec((1, tq, V_DIM), lambda b, h, i: (b, i, h)),
        scratch_shapes=[pltpu.VMEM((V_DIM, tq), F32), pltpu.VMEM((2, tk, tq), F32),
                        pltpu.VMEM((2, tk, tq), BF16)],
        compiler_params=_params("parallel", "parallel", "arbitrary"),
        name="flash",
    )(qT, k, vT, k_meta, vT_meta, sz)


def _out_kernel(x_ref, oz_ref, u_ref, up_ref, un_ref, um_ref, bz_ref, ga_ref, gc_ref,
                cw_ref, wa_ref, wc_ref, wo_ref, fw_ref, y_ref, *, tiles_per_seq, n_meta):
    i = pl.program_id(0)
    tm = u_ref.shape[0]
    pos = i % tiles_per_seq
    u = u_ref[...].astype(F32)
    halo = up_ref.shape[0]
    prev_row = jnp.where(pos == 0,
                         um_ref[...].astype(F32)[n_meta - 1:n_meta],
                         up_ref[...].astype(F32)[halo - 1:halo])
    next_row = jnp.where(pos == tiles_per_seq - 1,
                         jnp.zeros((1, u.shape[1]), F32),
                         un_ref[...].astype(F32)[0:1])
    rid = lax.broadcasted_iota(jnp.int32, u.shape, 0)
    u_prev = jnp.where(rid == 0, prev_row, pltpu.roll(u, 1, 0))
    u_next = jnp.where(rid == tm - 1, next_row, pltpu.roll(u, tm - 1, 0))
    cw = cw_ref[...]
    conv = cw[0:1] * u_prev + cw[1:2] * u + cw[2:3] * u_next
    yc_in = (bz_ref[...].astype(F32) * conv).astype(BF16)
    y_conv = jnp.dot(yc_in, wc_ref[...], preferred_element_type=F32)
    y_attn = jnp.dot(oz_ref[...], wa_ref[...], preferred_element_type=F32)
    merged = ga_ref[...].astype(F32) * y_attn + gc_ref[...].astype(F32) * y_conv
    out = x_ref[...] + jnp.dot(merged.astype(BF16), wo_ref[...], preferred_element_type=F32)
    y_ref[...] = _rms(out, fw_ref[...])


def _out(x, oz, u, u_meta, bz, g, w, *, seq_len, n_meta, tm_pref=256):
    t, d = x.shape
    cw = u.shape[1]
    aw = oz.shape[1]
    tm = _tile(seq_len, tm_pref)
    halo = 16
    hb = tm // halo
    n_halo = t // halo
    row = lambda i: (i, 0)
    kern = functools.partial(_out_kernel, tiles_per_seq=seq_len // tm, n_meta=n_meta)
    return pl.pallas_call(
        kern,
        out_shape=jax.ShapeDtypeStruct((t, d), F32),
        grid=(t // tm,),
        in_specs=[pl.BlockSpec((tm, d), row),
                  pl.BlockSpec((tm, aw), row),
                  pl.BlockSpec((tm, cw), row),
                  pl.BlockSpec((halo, cw), lambda i: (jnp.maximum(i * hb - 1, 0), 0)),
                  pl.BlockSpec((halo, cw), lambda i: (jnp.minimum((i + 1) * hb, n_halo - 1), 0)),
                  pl.BlockSpec((halo, cw), lambda i: (0, 0)),
                  pl.BlockSpec((tm, cw), row),
                  pl.BlockSpec((tm, d), lambda i: (i, 0)),
                  pl.BlockSpec((tm, d), lambda i: (i, 1)),
                  _resident(w["conv_w"].shape),
                  _resident(w["w_o_attn"].shape),
                  _resident(w["w_o_conv"].shape),
                  _resident(w["w_o"].shape),
                  _resident((1, d))],
        out_specs=pl.BlockSpec((tm, d), row),
        compiler_params=_params("parallel"),
        name="out",
    )(x, oz, u, u, u, u_meta, bz, g, g, w["conv_w"], w["w_o_attn"], w["w_o_conv"], w["w_o"],
      w["final_norm"])


def _prep_weights(norm_w, w_in, b_gate, q_a_norm_w, w_uq, kv_a_norm_w, w_ukv,
                  w_o_attn, conv_w, w_o_conv, w_o, final_norm_w):
    d = w_in.shape[1]
    q_rank = q_a_norm_w.shape[-1]
    kv_rank = kv_a_norm_w.shape[-1]
    n_heads = w_uq.shape[-1] // QK_DIM
    aw = w_o_attn.shape[1]
    cwid = w_o_conv.shape[1]
    sizes = (q_rank, kv_rank, QK_ROPE_DIM, aw, cwid, cwid, cwid, cwid, 2 * d)
    assert sum(sizes) == w_in.shape[-1]
    cols, start = [], 0
    for sz in sizes:
        cols.append((start, start + sz))
        start += sz
    win = w_in[0]
    cut = lambda k: win[:, cols[k][0]:cols[k][1]].astype(BF16)
    wuq = w_uq[0].reshape(q_rank, n_heads, QK_DIM).transpose(1, 2, 0)
    wukv = w_ukv[0].reshape(kv_rank, n_heads, QK_NOPE_DIM + V_DIM)
    return dict(
        n_heads=n_heads, q_rank=q_rank, kv_rank=kv_rank,
        norm=norm_w[0],
        w_small=win[:, :cols[2][1]].astype(BF16),
        w_z=cut(3), w_cx=cut(4), w_cb=cut(5), w_cc=cut(6), w_zc=cut(7), w_g=cut(8),
        b_gate=b_gate[0],
        q_norm=q_a_norm_w[0].reshape(1, q_rank),
        kv_norm=kv_a_norm_w[0].reshape(1, kv_rank),
        w_uqT=wuq.reshape(n_heads * QK_DIM, q_rank).astype(BF16),
        w_uk=wukv[:, :, :QK_NOPE_DIM].reshape(kv_rank, n_heads * QK_NOPE_DIM).astype(BF16),
        w_uvT=wukv[:, :, QK_NOPE_DIM:].transpose(1, 2, 0).reshape(n_heads * V_DIM, kv_rank).astype(BF16),
        w_o_attn=w_o_attn[0].astype(BF16),
        conv_w=conv_w[0],
        w_o_conv=w_o_conv[0].astype(BF16),
        w_o=w_o[0].astype(BF16),
        final_norm=final_norm_w.reshape(1, d),
    )


def _meta_state(meta_tokens, w):
    n_meta, d = meta_tokens.shape
    xm = jnp.pad(meta_tokens, ((0, META_PAD - n_meta), (0, 0)))
    xn = _rmsnorm(xm, w["norm"])
    _, k_meta, vT_meta = _attn_prep(xn[None], w, _rope_tables(0, META_PAD))
    u_meta = _proj(xn, w["w_cx"], w["w_cc"], "mul")
    return k_meta[0], vT_meta[0], u_meta


def _trunk(x, meta, w, n_meta):
    bsz, s, d = x.shape
    k_meta, vT_meta, u_meta = meta
    x2 = x.reshape(bsz * s, d)
    xn = _rmsnorm(x2, w["norm"])
    qT, k, vT = _attn_prep(xn.reshape(bsz, s, d), w, _rope_tables(n_meta, s))
    sz = _proj(xn, w["w_z"], None, "silu")
    u = _proj(xn, w["w_cx"], w["w_cc"], "mul")
    bz = _proj(xn, w["w_cb"], w["w_zc"], "mul_silu")
    g = _proj(xn, w["w_g"], w["b_gate"], "sigmoid_bias")
    oz = _flash(qT, k, vT, k_meta, vT_meta, sz.reshape(bsz, s, -1), n_meta=n_meta)
    y = _out(x2, oz.reshape(bsz * s, -1), u, u_meta, bz, g, w, seq_len=s, n_meta=n_meta)
    return y.reshape(bsz, s, d)


def kernel(x_prompt, x_sample, meta_tokens, norm_w, w_in, b_gate, q_a_norm_w, w_uq, kv_a_norm_w, w_ukv, w_o_attn, conv_w, w_o_conv, w_o, final_norm_w):
    assert norm_w.shape[0] == 1
    n_meta = meta_tokens.shape[0]
    assert n_meta <= 16
    w = _prep_weights(norm_w, w_in, b_gate, q_a_norm_w, w_uq, kv_a_norm_w, w_ukv,
                      w_o_attn, conv_w, w_o_conv, w_o, final_norm_w)
    meta = _meta_state(meta_tokens, w)
    return (_trunk(x_prompt, meta, w, n_meta), _trunk(x_sample, meta, w, n_meta))
```

```python
import functools

import jax
import jax.numpy as jnp
from jax import lax
from jax.experimental import pallas as pl
from jax.experimental.pallas import tpu as pltpu

QK_NOPE_DIM = 128
QK_ROPE_DIM = 64
QK_DIM = QK_NOPE_DIM + QK_ROPE_DIM
V_DIM = 128
ROPE_HALF = QK_ROPE_DIM // 2
QK_PAD = 256
ROPE_THETA = 10000.0
NORM_EPS = 1e-6
MASK_VALUE = -1e30
LOG2_E = 1.4426950408889634
META_PAD = 128
VMEM_LIMIT_BYTES = 56 * 1024 * 1024

F32 = jnp.float32
BF16 = jnp.bfloat16


def _params(*semantics):
    return pltpu.CompilerParams(dimension_semantics=semantics,
                                vmem_limit_bytes=VMEM_LIMIT_BYTES)


def _tile(n, pref):
    if n <= pref:
        return n
    t = pref
    while n % t:
        t //= 2
    return t


def _nt_dot(a, b):
    return lax.dot_general(a, b, (((1,), (1,)), ((), ())), preferred_element_type=F32)


def _rms(x, w):
    return x * lax.rsqrt(jnp.mean(x * x, axis=-1, keepdims=True) + NORM_EPS) * w


def _resident(shape):
    nd = len(shape)
    return pl.BlockSpec(shape, lambda *_: (0,) * nd, pipeline_mode=pl.Buffered(1))


def _rmsnorm_kernel(x_ref, w_ref, o_ref):
    o_ref[...] = _rms(x_ref[...], w_ref[...]).astype(o_ref.dtype)


def _rmsnorm(x, w):
    t, d = x.shape
    tm = _tile(t, 512)
    return pl.pallas_call(
        _rmsnorm_kernel,
        out_shape=jax.ShapeDtypeStruct((t, d), BF16),
        grid=(t // tm,),
        in_specs=[pl.BlockSpec((tm, d), lambda i: (i, 0)),
                  pl.BlockSpec((1, d), lambda i: (0, 0))],
        out_specs=pl.BlockSpec((tm, d), lambda i: (i, 0)),
        compiler_params=_params("parallel"),
        name="rmsnorm",
    )(x, w.reshape(1, d))


def _attn_prep_kernel(xn_ref, wsm_ref, qnw_ref, kvnw_ref, wuqT_ref, wuk_ref, wuvT_ref,
                      cosT_ref, sinT_ref, cosK_ref, sinK_ref,
                      qT_ref, k_ref, vT_ref, *, n_heads, q_rank, kv_rank, scale):
    tm = xn_ref.shape[1]
    p = jnp.dot(xn_ref[0], wsm_ref[...], preferred_element_type=F32)
    qn = _rms(p[:, :q_rank], qnw_ref[...]).astype(BF16)
    cn = _rms(p[:, q_rank:q_rank + kv_rank], kvnw_ref[...]).astype(BF16)
    kr = p[:, q_rank + kv_rank:]
    kr_sw = jnp.concatenate([kr[:, ROPE_HALF:], kr[:, :ROPE_HALF]], axis=1)
    kr_rot = (kr * cosK_ref[...] + kr_sw * sinK_ref[...]).astype(BF16)

    qT = _nt_dot(wuqT_ref[...], qn) * scale
    k_nope = jnp.dot(cn, wuk_ref[...], preferred_element_type=F32)
    vT = _nt_dot(wuvT_ref[...], cn)
    c = cosT_ref[...]
    s = sinT_ref[...]
    q_zero = jnp.zeros((QK_PAD - QK_DIM, tm), BF16)
    k_zero = jnp.zeros((tm, QK_PAD - QK_DIM), BF16)
    r1 = QK_NOPE_DIM + ROPE_HALF
    for h in range(n_heads):
        b = h * QK_DIM
        x1 = qT[b + QK_NOPE_DIM:b + r1]
        x2 = qT[b + r1:b + QK_DIM]
        qT_ref[0, h, 0:QK_NOPE_DIM, :] = qT[b:b + QK_NOPE_DIM].astype(BF16)
        qT_ref[0, h, QK_NOPE_DIM:r1, :] = (x1 * c - x2 * s).astype(BF16)
        qT_ref[0, h, r1:QK_DIM, :] = (x2 * c + x1 * s).astype(BF16)
        qT_ref[0, h, QK_DIM:QK_PAD, :] = q_zero
        k_ref[0, h, :, 0:QK_NOPE_DIM] = k_nope[:, h * QK_NOPE_DIM:(h + 1) * QK_NOPE_DIM].astype(BF16)
        k_ref[0, h, :, QK_NOPE_DIM:QK_DIM] = kr_rot
        k_ref[0, h, :, QK_DIM:QK_PAD] = k_zero
        vT_ref[0, h] = vT[h * V_DIM:(h + 1) * V_DIM].astype(BF16)


def _attn_prep(xn, w, rope, *, tm_pref=256):
    bsz, s, d = xn.shape
    n_heads = w["n_heads"]
    q_rank, kv_rank = w["q_rank"], w["kv_rank"]
    tm = _tile(s, tm_pref)
    cosT, sinT, cosK, sinK = rope
    kern = functools.partial(_attn_prep_kernel, n_heads=n_heads, q_rank=q_rank,
                             kv_rank=kv_rank, scale=QK_DIM ** -0.5 * LOG2_E)
    return pl.pallas_call(
        kern,
        out_shape=(jax.ShapeDtypeStruct((bsz, n_heads, QK_PAD, s), BF16),
                   jax.ShapeDtypeStruct((bsz, n_heads, s, QK_PAD), BF16),
                   jax.ShapeDtypeStruct((bsz, n_heads, V_DIM, s), BF16)),
        grid=(bsz, s // tm),
        in_specs=[pl.BlockSpec((1, tm, d), lambda b, i: (b, i, 0)),
                  _resident(w["w_small"].shape),
                  _resident((1, q_rank)),
                  _resident((1, kv_rank)),
                  _resident(w["w_uqT"].shape),
                  _resident(w["w_uk"].shape),
                  _resident(w["w_uvT"].shape),
                  pl.BlockSpec((ROPE_HALF, tm), lambda b, i: (0, i)),
                  pl.BlockSpec((ROPE_HALF, tm), lambda b, i: (0, i)),
                  pl.BlockSpec((tm, QK_ROPE_DIM), lambda b, i: (i, 0)),
                  pl.BlockSpec((tm, QK_ROPE_DIM), lambda b, i: (i, 0))],
        out_specs=(pl.BlockSpec((1, n_heads, QK_PAD, tm), lambda b, i: (b, 0, 0, i)),
                   pl.BlockSpec((1, n_heads, tm, QK_PAD), lambda b, i: (b, 0, i, 0)),
                   pl.BlockSpec((1, n_heads, V_DIM, tm), lambda b, i: (b, 0, 0, i))),
        compiler_params=_params("parallel", "parallel"),
        name="attn_prep",
    )(xn, w["w_small"], w["q_norm"], w["kv_norm"], w["w_uqT"], w["w_uk"], w["w_uvT"],
      cosT, sinT, cosK, sinK)


def _rope_tables(start, length):
    inv_freq = 1.0 / (ROPE_THETA ** (jnp.arange(0, QK_ROPE_DIM, 2, dtype=F32) / QK_ROPE_DIM))
    pos = jnp.arange(start, start + length, dtype=F32)
    ang = pos[:, None] * inv_freq[None, :]
    cos, sin = jnp.cos(ang), jnp.sin(ang)
    cosK = jnp.concatenate([cos, cos], axis=1)
    sinK = jnp.concatenate([-sin, sin], axis=1)
    return cos.T, sin.T, cosK, sinK


def _proj_kernel(*refs, mode):
    xn_ref, w1_ref = refs[0], refs[1]
    o_ref = refs[-1]
    xn = xn_ref[...]
    p1 = jnp.dot(xn, w1_ref[...], preferred_element_type=F32)
    if mode == "silu":
        out = jax.nn.silu(p1)
    elif mode == "sigmoid_bias":
        out = jax.nn.sigmoid(p1 + refs[2][...])
    else:
        p2 = jnp.dot(xn, refs[2][...], preferred_element_type=F32)
        out = p1 * p2 if mode == "mul" else p1 * jax.nn.silu(p2)
    o_ref[...] = out.astype(o_ref.dtype)


def _proj(xn, w1, second, mode, *, tm_pref=1024, tn_pref=1024):
    t, d = xn.shape
    n = w1.shape[1]
    single = mode in ("silu", "sigmoid_bias")
    tm, tn = _tile(t, tm_pref), _tile(n, 2 * tn_pref if single else tn_pref)
    w_spec = pl.BlockSpec((d, tn), lambda i, j: (0, j))
    in_specs = [pl.BlockSpec((tm, d), lambda i, j: (i, 0)), w_spec]
    args = [xn, w1]
    if mode == "sigmoid_bias":
        in_specs.append(pl.BlockSpec((1, tn), lambda i, j: (0, j)))
        args.append(second.reshape(1, n))
    elif mode != "silu":
        in_specs.append(w_spec)
        args.append(second)
    return pl.pallas_call(
        functools.partial(_proj_kernel, mode=mode),
        out_shape=jax.ShapeDtypeStruct((t, n), BF16),
        grid=(t // tm, n // tn),
        in_specs=in_specs,
        out_specs=pl.BlockSpec((tm, tn), lambda i, j: (i, j)),
        compiler_params=_params("parallel", "parallel"),
        name="proj_" + mode,
    )(*args)


def _flash_kernel(qT_ref, k_ref, vT_ref, km_ref, vTm_ref, sz_ref, o_ref, acc_ref, s_ref,
                  *, tk, n_meta):
    qT = qT_ref[0, 0]
    n_pairs = k_ref.shape[2] // (2 * tk)

    def scores(chunk, slot):
        off = pl.multiple_of(chunk * tk, tk)
        s = jnp.dot(k_ref[0, 0, pl.ds(off, tk), :], qT, preferred_element_type=F32)
        s_ref[slot] = s
        return jnp.max(s, axis=0, keepdims=True)

    def consume(chunk, slot, mx, m, l):
        off = pl.multiple_of(chunk * tk, tk)
        m_new = jnp.maximum(m, mx)
        alpha = jnp.exp2(m - m_new)
        p = jnp.exp2(s_ref[slot] - m_new)
        l = alpha * l + jnp.sum(p, axis=0, keepdims=True)
        pv = jnp.dot(vT_ref[0, 0, :, pl.ds(off, tk)], p.astype(BF16),
                     preferred_element_type=F32)
        acc_ref[...] = acc_ref[...] * alpha + pv
        return m_new, l

    mx0 = scores(0, 0)
    s = jnp.dot(km_ref[0], qT, preferred_element_type=F32)
    row = lax.broadcasted_iota(jnp.int32, s.shape, 0)
    s = jnp.where(row < n_meta, s, MASK_VALUE)
    m = jnp.max(s, axis=0, keepdims=True)
    p = jnp.exp2(s - m)
    l = jnp.sum(p, axis=0, keepdims=True)
    acc_ref[...] = jnp.dot(vTm_ref[0], p.astype(BF16), preferred_element_type=F32)

    def pair(jj, carry):
        mx0, m, l = carry
        a = 2 * jj
        mx1 = scores(a + 1, 1)
        m, l = consume(a, 0, mx0, m, l)
        mx0 = scores(a + 2, 0)
        m, l = consume(a + 1, 1, mx1, m, l)
        return mx0, m, l

    n_trips = n_pairs - 1 + jnp.minimum(pl.program_id(2), 0)
    mx0, m, l = lax.fori_loop(0, n_trips, pair, (mx0, m, l))
    a = 2 * (n_pairs - 1)
    mx1 = scores(a + 1, 1)
    m, l = consume(a, 0, mx0, m, l)
    m, l = consume(a + 1, 1, mx1, m, l)
    o = (acc_ref[...] / l).T
    o_ref[0] = (o * sz_ref[0].astype(F32)).astype(o_ref.dtype)


def _flash(qT, k, vT, k_meta, vT_meta, sz, *, n_meta, tq_pref=2048, tk_pref=1024):
    bsz, n_heads, _, s = qT.shape
    tq = _tile(s, tq_pref)
    tk = _tile(s // 2, tk_pref)
    return pl.pallas_call(
        functools.partial(_flash_kernel, tk=tk, n_meta=n_meta),
        out_shape=jax.ShapeDtypeStruct((bsz, s, n_heads * V_DIM), BF16),
        grid=(bsz, n_heads, s // tq),
        in_specs=[pl.BlockSpec((1, 1, QK_PAD, tq), lambda b, h, i: (b, h, 0, i)),
                  pl.BlockSpec((1, 1, s, QK_PAD), lambda b, h, i: (b, h, 0, 0)),
                  pl.BlockSpec((1, 1, V_DIM, s), lambda b, h, i: (b, h, 0, 0)),
                  pl.BlockSpec((1, META_PAD, QK_PAD), lambda b, h, i: (h, 0, 0)),
                  pl.BlockSpec((1, V_DIM, META_PAD), lambda b, h, i: (h, 0, 0)),
                  pl.BlockSpec((1, tq, V_DIM), lambda b, h, i: (b, i, h))],
        out_specs=pl.BlockSpec((1, tq, V_DIM), lambda b, h, i: (b, i, h)),
        scratch_shapes=[pltpu.VMEM((V_DIM, tq), F32), pltpu.VMEM((2, tk, tq), F32)],
        compiler_params=_params("parallel", "parallel", "arbitrary"),
        name="flash",
    )(qT, k, vT, k_meta, vT_meta, sz)


def _out_kernel(x_ref, oz_ref, u_ref, up_ref, un_ref, um_ref, bz_ref, ga_ref, gc_ref,
                cw_ref, wa_ref, wc_ref, wo_ref, fw_ref, y_ref, *, tiles_per_seq, n_meta):
    i = pl.program_id(0)
    tm = u_ref.shape[0]
    pos = i % tiles_per_seq
    u = u_ref[...].astype(F32)
    halo = up_ref.shape[0]
    prev_row = jnp.where(pos == 0,
                         um_ref[...].astype(F32)[n_meta - 1:n_meta],
                         up_ref[...].astype(F32)[halo - 1:halo])
    next_row = jnp.where(pos == tiles_per_seq - 1,
                         jnp.zeros((1, u.shape[1]), F32),
                         un_ref[...].astype(F32)[0:1])
    rid = lax.broadcasted_iota(jnp.int32, u.shape, 0)
    u_prev = jnp.where(rid == 0, prev_row, pltpu.roll(u, 1, 0))
    u_next = jnp.where(rid == tm - 1, next_row, pltpu.roll(u, tm - 1, 0))
    cw = cw_ref[...]
    conv = cw[0:1] * u_prev + cw[1:2] * u + cw[2:3] * u_next
    yc_in = (bz_ref[...].astype(F32) * conv).astype(BF16)
    y_conv = jnp.dot(yc_in, wc_ref[...], preferred_element_type=F32)
    y_attn = jnp.dot(oz_ref[...], wa_ref[...], preferred_element_type=F32)
    merged = ga_ref[...].astype(F32) * y_attn + gc_ref[...].astype(F32) * y_conv
    out = x_ref[...] + jnp.dot(merged.astype(BF16), wo_ref[...], preferred_element_type=F32)
    y_ref[...] = _rms(out, fw_ref[...])


def _out(x, oz, u, u_meta, bz, g, w, *, seq_len, n_meta, tm_pref=256):
    t, d = x.shape
    cw = u.shape[1]
    aw = oz.shape[1]
    tm = _tile(seq_len, tm_pref)
    halo = 16
    hb = tm // halo
    n_halo = t // halo
    row = lambda i: (i, 0)
    kern = functools.partial(_out_kernel, tiles_per_seq=seq_len // tm, n_meta=n_meta)
    return pl.pallas_call(
        kern,
        out_shape=jax.ShapeDtypeStruct((t, d), F32),
        grid=(t // tm,),
        in_specs=[pl.BlockSpec((tm, d), row),
                  pl.BlockSpec((tm, aw), row),
                  pl.BlockSpec((tm, cw), row),
                  pl.BlockSpec((halo, cw), lambda i: (jnp.maximum(i * hb - 1, 0), 0)),
                  pl.BlockSpec((halo, cw), lambda i: (jnp.minimum((i + 1) * hb, n_halo - 1), 0)),
                  pl.BlockSpec((halo, cw), lambda i: (0, 0)),
                  pl.BlockSpec((tm, cw), row),
                  pl.BlockSpec((tm, d), lambda i: (i, 0)),
                  pl.BlockSpec((tm, d), lambda i: (i, 1)),
                  _resident(w["conv_w"].shape),
                  _resident(w["w_o_attn"].shape),
                  _resident(w["w_o_conv"].shape),
                  _resident(w["w_o"].shape),
                  _resident((1, d))],
        out_specs=pl.BlockSpec((tm, d), row),
        compiler_params=_params("parallel"),
        name="out",
    )(x, oz, u, u, u, u_meta, bz, g, g, w["conv_w"], w["w_o_attn"], w["w_o_conv"], w["w_o"],
      w["final_norm"])


def _prep_weights(norm_w, w_in, b_gate, q_a_norm_w, w_uq, kv_a_norm_w, w_ukv,
                  w_o_attn, conv_w, w_o_conv, w_o, final_norm_w):
    d = w_in.shape[1]
    q_rank = q_a_norm_w.shape[-1]
    kv_rank = kv_a_norm_w.shape[-1]
    n_heads = w_uq.shape[-1] // QK_DIM
    aw = w_o_attn.shape[1]
    cwid = w_o_conv.shape[1]
    sizes = (q_rank, kv_rank, QK_ROPE_DIM, aw, cwid, cwid, cwid, cwid, 2 * d)
    assert sum(sizes) == w_in.shape[-1]
    cols, start = [], 0
    for sz in sizes:
        cols.append((start, start + sz))
        start += sz
    win = w_in[0]
    cut = lambda k: win[:, cols[k][0]:cols[k][1]].astype(BF16)
    wuq = w_uq[0].reshape(q_rank, n_heads, QK_DIM).transpose(1, 2, 0)
    wukv = w_ukv[0].reshape(kv_rank, n_heads, QK_NOPE_DIM + V_DIM)
    return dict(
        n_heads=n_heads, q_rank=q_rank, kv_rank=kv_rank,
        norm=norm_w[0],
        w_small=win[:, :cols[2][1]].astype(BF16),
        w_z=cut(3), w_cx=cut(4), w_cb=cut(5), w_cc=cut(6), w_zc=cut(7), w_g=cut(8),
        b_gate=b_gate[0],
        q_norm=q_a_norm_w[0].reshape(1, q_rank),
        kv_norm=kv_a_norm_w[0].reshape(1, kv_rank),
        w_uqT=wuq.reshape(n_heads * QK_DIM, q_rank).astype(BF16),
        w_uk=wukv[:, :, :QK_NOPE_DIM].reshape(kv_rank, n_heads * QK_NOPE_DIM).astype(BF16),
        w_uvT=wukv[:, :, QK_NOPE_DIM:].transpose(1, 2, 0).reshape(n_heads * V_DIM, kv_rank).astype(BF16),
        w_o_attn=w_o_attn[0].astype(BF16),
        conv_w=conv_w[0],
        w_o_conv=w_o_conv[0].astype(BF16),
        w_o=w_o[0].astype(BF16),
        final_norm=final_norm_w.reshape(1, d),
    )


def _meta_state(meta_tokens, w):
    n_meta, d = meta_tokens.shape
    xm = jnp.pad(meta_tokens, ((0, META_PAD - n_meta), (0, 0)))
    xn = _rmsnorm(xm, w["norm"])
    _, k_meta, vT_meta = _attn_prep(xn[None], w, _rope_tables(0, META_PAD))
    u_meta = _proj(xn, w["w_cx"], w["w_cc"], "mul")
    return k_meta[0], vT_meta[0], u_meta


def _trunk(x, meta, w, n_meta):
    bsz, s, d = x.shape
    k_meta, vT_meta, u_meta = meta
    x2 = x.reshape(bsz * s, d)
    xn = _rmsnorm(x2, w["norm"])
    qT, k, vT = _attn_prep(xn.reshape(bsz, s, d), w, _rope_tables(n_meta, s))
    sz = _proj(xn, w["w_z"], None, "silu")
    u = _proj(xn, w["w_cx"], w["w_cc"], "mul")
    bz = _proj(xn, w["w_cb"], w["w_zc"], "mul_silu")
    g = _proj(xn, w["w_g"], w["b_gate"], "sigmoid_bias")
    oz = _flash(qT, k, vT, k_meta, vT_meta, sz.reshape(bsz, s, -1), n_meta=n_meta)
    y = _out(x2, oz.reshape(bsz * s, -1), u, u_meta, bz, g, w, seq_len=s, n_meta=n_meta)
    return y.reshape(bsz, s, d)


def kernel(x_prompt, x_sample, meta_tokens, norm_w, w_in, b_gate, q_a_norm_w, w_uq, kv_a_norm_w, w_ukv, w_o_attn, conv_w, w_o_conv, w_o, final_norm_w):
    assert norm_w.shape[0] == 1
    n_meta = meta_tokens.shape[0]
    assert n_meta <= 16
    w = _prep_weights(norm_w, w_in, b_gate, q_a_norm_w, w_uq, kv_a_norm_w, w_ukv,
                      w_o_attn, conv_w, w_o_conv, w_o, final_norm_w)
    meta = _meta_state(meta_tokens, w)
    return (_trunk(x_prompt, meta, w, n_meta), _trunk(x_sample, meta, w, n_meta))
```

```python
import functools

import jax
import jax.numpy as jnp
from jax import lax
from jax.experimental import pallas as pl
from jax.experimental.pallas import tpu as pltpu

QK_NOPE_DIM = 128
QK_ROPE_DIM = 64
QK_DIM = QK_NOPE_DIM + QK_ROPE_DIM
V_DIM = 128
ROPE_HALF = QK_ROPE_DIM // 2
QK_PAD = 256
ROPE_THETA = 10000.0
NORM_EPS = 1e-6
MASK_VALUE = -1e30
MAX_JUMP = 64.0
LOG2_E = 1.4426950408889634
META_PAD = 128
VMEM_LIMIT_BYTES = 56 * 1024 * 1024

F32 = jnp.float32
BF16 = jnp.bfloat16


def _params(*semantics):
    return pltpu.CompilerParams(dimension_semantics=semantics,
                                vmem_limit_bytes=VMEM_LIMIT_BYTES)


def _tile(n, pref):
    if n <= pref:
        return n
    t = pref
    while n % t:
        t //= 2
    return t


def _nt_dot(a, b):
    return lax.dot_general(a, b, (((1,), (1,)), ((), ())), preferred_element_type=F32)


def _rms(x, w):
    return x * lax.rsqrt(jnp.mean(x * x, axis=-1, keepdims=True) + NORM_EPS) * w


def _resident(shape):
    nd = len(shape)
    return pl.BlockSpec(shape, lambda *_: (0,) * nd, pipeline_mode=pl.Buffered(1))


def _rmsnorm_kernel(x_ref, w_ref, o_ref):
    o_ref[...] = _rms(x_ref[...], w_ref[...]).astype(o_ref.dtype)


def _rmsnorm(x, w):
    t, d = x.shape
    tm = _tile(t, 512)
    return pl.pallas_call(
        _rmsnorm_kernel,
        out_shape=jax.ShapeDtypeStruct((t, d), BF16),
        grid=(t // tm,),
        in_specs=[pl.BlockSpec((tm, d), lambda i: (i, 0)),
                  pl.BlockSpec((1, d), lambda i: (0, 0))],
        out_specs=pl.BlockSpec((tm, d), lambda i: (i, 0)),
        compiler_params=_params("parallel"),
        name="rmsnorm",
    )(x, w.reshape(1, d))


def _attn_prep_kernel(xn_ref, wsm_ref, qnw_ref, kvnw_ref, wuqT_ref, wuk_ref, wuvT_ref,
                      cosT_ref, sinT_ref, cosK_ref, sinK_ref,
                      qT_ref, k_ref, vT_ref, *, n_heads, q_rank, kv_rank, scale):
    tm = xn_ref.shape[1]
    p = jnp.dot(xn_ref[0], wsm_ref[...], preferred_element_type=F32)
    qn = _rms(p[:, :q_rank], qnw_ref[...]).astype(BF16)
    cn = _rms(p[:, q_rank:q_rank + kv_rank], kvnw_ref[...]).astype(BF16)
    kr = p[:, q_rank + kv_rank:]
    kr_sw = jnp.concatenate([kr[:, ROPE_HALF:], kr[:, :ROPE_HALF]], axis=1)
    kr_rot = (kr * cosK_ref[...] + kr_sw * sinK_ref[...]).astype(BF16)

    qT = _nt_dot(wuqT_ref[...], qn) * scale
    k_nope = jnp.dot(cn, wuk_ref[...], preferred_element_type=F32)
    vT = _nt_dot(wuvT_ref[...], cn)
    c = cosT_ref[...]
    s = sinT_ref[...]
    q_zero = jnp.zeros((QK_PAD - QK_DIM, tm), BF16)
    k_zero = jnp.zeros((tm, QK_PAD - QK_DIM), BF16)
    r1 = QK_NOPE_DIM + ROPE_HALF
    for h in range(n_heads):
        b = h * QK_DIM
        x1 = qT[b + QK_NOPE_DIM:b + r1]
        x2 = qT[b + r1:b + QK_DIM]
        qT_ref[0, h, 0:QK_NOPE_DIM, :] = qT[b:b + QK_NOPE_DIM].astype(BF16)
        qT_ref[0, h, QK_NOPE_DIM:r1, :] = (x1 * c - x2 * s).astype(BF16)
        qT_ref[0, h, r1:QK_DIM, :] = (x2 * c + x1 * s).astype(BF16)
        qT_ref[0, h, QK_DIM:QK_PAD, :] = q_zero
        k_ref[0, h, :, 0:QK_NOPE_DIM] = k_nope[:, h * QK_NOPE_DIM:(h + 1) * QK_NOPE_DIM].astype(BF16)
        k_ref[0, h, :, QK_NOPE_DIM:QK_DIM] = kr_rot
        k_ref[0, h, :, QK_DIM:QK_PAD] = k_zero
        vT_ref[0, h] = vT[h * V_DIM:(h + 1) * V_DIM].astype(BF16)


def _attn_prep(xn, w, rope, *, tm_pref=256):
    bsz, s, d = xn.shape
    n_heads = w["n_heads"]
    q_rank, kv_rank = w["q_rank"], w["kv_rank"]
    tm = _tile(s, tm_pref)
    cosT, sinT, cosK, sinK = rope
    kern = functools.partial(_attn_prep_kernel, n_heads=n_heads, q_rank=q_rank,
                             kv_rank=kv_rank, scale=QK_DIM ** -0.5 * LOG2_E)
    return pl.pallas_call(
        kern,
        out_shape=(jax.ShapeDtypeStruct((bsz, n_heads, QK_PAD, s), BF16),
                   jax.ShapeDtypeStruct((bsz, n_heads, s, QK_PAD), BF16),
                   jax.ShapeDtypeStruct((bsz, n_heads, V_DIM, s), BF16)),
        grid=(bsz, s // tm),
        in_specs=[pl.BlockSpec((1, tm, d), lambda b, i: (b, i, 0)),
                  _resident(w["w_small"].shape),
                  _resident((1, q_rank)),
                  _resident((1, kv_rank)),
                  _resident(w["w_uqT"].shape),
                  _resident(w["w_uk"].shape),
                  _resident(w["w_uvT"].shape),
                  pl.BlockSpec((ROPE_HALF, tm), lambda b, i: (0, i)),
                  pl.BlockSpec((ROPE_HALF, tm), lambda b, i: (0, i)),
                  pl.BlockSpec((tm, QK_ROPE_DIM), lambda b, i: (i, 0)),
                  pl.BlockSpec((tm, QK_ROPE_DIM), lambda b, i: (i, 0))],
        out_specs=(pl.BlockSpec((1, n_heads, QK_PAD, tm), lambda b, i: (b, 0, 0, i)),
                   pl.BlockSpec((1, n_heads, tm, QK_PAD), lambda b, i: (b, 0, i, 0)),
                   pl.BlockSpec((1, n_heads, V_DIM, tm), lambda b, i: (b, 0, 0, i))),
        compiler_params=_params("parallel", "parallel"),
        name="attn_prep",
    )(xn, w["w_small"], w["q_norm"], w["kv_norm"], w["w_uqT"], w["w_uk"], w["w_uvT"],
      cosT, sinT, cosK, sinK)


def _rope_tables(start, length):
    inv_freq = 1.0 / (ROPE_THETA ** (jnp.arange(0, QK_ROPE_DIM, 2, dtype=F32) / QK_ROPE_DIM))
    pos = jnp.arange(start, start + length, dtype=F32)
    ang = pos[:, None] * inv_freq[None, :]
    cos, sin = jnp.cos(ang), jnp.sin(ang)
    cosK = jnp.concatenate([cos, cos], axis=1)
    sinK = jnp.concatenate([-sin, sin], axis=1)
    return cos.T, sin.T, cosK, sinK


def _proj_kernel(*refs, mode):
    xn_ref, w1_ref = refs[0], refs[1]
    o_ref = refs[-1]
    xn = xn_ref[...]
    p1 = jnp.dot(xn, w1_ref[...], preferred_element_type=F32)
    if mode == "silu":
        out = jax.nn.silu(p1)
    elif mode == "sigmoid_bias":
        out = jax.nn.sigmoid(p1 + refs[2][...])
    else:
        p2 = jnp.dot(xn, refs[2][...], preferred_element_type=F32)
        out = p1 * p2 if mode == "mul" else p1 * jax.nn.silu(p2)
    o_ref[...] = out.astype(o_ref.dtype)


def _proj(xn, w1, second, mode, *, tm_pref=1024, tn_pref=1024):
    t, d = xn.shape
    n = w1.shape[1]
    single = mode in ("silu", "sigmoid_bias")
    tm, tn = _tile(t, tm_pref), _tile(n, 2 * tn_pref if single else tn_pref)
    w_spec = pl.BlockSpec((d, tn), lambda i, j: (0, j))
    in_specs = [pl.BlockSpec((tm, d), lambda i, j: (i, 0)), w_spec]
    args = [xn, w1]
    if mode == "sigmoid_bias":
        in_specs.append(pl.BlockSpec((1, tn), lambda i, j: (0, j)))
        args.append(second.reshape(1, n))
    elif mode != "silu":
        in_specs.append(w_spec)
        args.append(second)
    return pl.pallas_call(
        functools.partial(_proj_kernel, mode=mode),
        out_shape=jax.ShapeDtypeStruct((t, n), BF16),
        grid=(t // tm, n // tn),
        in_specs=in_specs,
        out_specs=pl.BlockSpec((tm, tn), lambda i, j: (i, j)),
        compiler_params=_params("parallel", "parallel"),
        name="proj_" + mode,
    )(*args)


def _flash_kernel(qT_ref, k_ref, vT_ref, km_ref, vTm_ref, sz_ref, o_ref, acc_ref, l_ref,
                  *, tk, n_meta):
    qT = qT_ref[0, 0]
    n_kv = k_ref.shape[2] // tk

    def meta_block():
        s = jnp.dot(km_ref[0], qT, preferred_element_type=F32)
        row = lax.broadcasted_iota(jnp.int32, s.shape, 0)
        s = jnp.where(row < n_meta, s, MASK_VALUE)
        m = jnp.max(s, axis=0, keepdims=True)
        p = jnp.exp2(s - m)
        acc_ref[...] = jnp.dot(vTm_ref[0], p.astype(BF16), preferred_element_type=F32)
        return m, jnp.sum(p, axis=0, keepdims=True)

    def chunk(c, carry, *, two_pass):
        m, l, jump = carry
        off = pl.multiple_of(c * tk, tk)
        s = jnp.dot(k_ref[0, 0, pl.ds(off, tk), :], qT, preferred_element_type=F32)
        m_new = jnp.maximum(m, jnp.max(s, axis=0, keepdims=True))
        alpha = jnp.exp2(m - m_new)
        p = jnp.exp2(s - (m_new if two_pass else m))
        ls = jnp.sum(p, axis=0, keepdims=True)
        pv = jnp.dot(vT_ref[0, 0, :, pl.ds(off, tk)], p.astype(BF16),
                     preferred_element_type=F32)
        if two_pass:
            acc_ref[...] = acc_ref[...] * alpha + pv
            l = l * alpha + ls
        else:
            acc_ref[...] = (acc_ref[...] + pv) * alpha
            l = (l + ls) * alpha
        return m_new, l, jnp.maximum(jump, m_new - m)

    def sweep(*, two_pass):
        m, l = meta_block()
        carry = (m, l, jnp.zeros_like(m))
        if two_pass:
            return lax.fori_loop(0, n_kv, functools.partial(chunk, two_pass=True), carry)

        def pair(jj, carry):
            carry = chunk(2 * jj, carry, two_pass=False)
            return chunk(2 * jj + 1, carry, two_pass=False)

        n_trips = n_kv // 2 + jnp.minimum(pl.program_id(2), 0)
        return lax.fori_loop(0, n_trips, pair, carry)

    _, l, jump = sweep(two_pass=False)
    l_ref[...] = l

    @pl.when(jnp.max(jump) > MAX_JUMP)
    def _():
        _, l, _ = sweep(two_pass=True)
        l_ref[...] = l

    o = (acc_ref[...] / l_ref[...]).T
    o_ref[0] = (o * sz_ref[0].astype(F32)).astype(o_ref.dtype)


def _flash(qT, k, vT, k_meta, vT_meta, sz, *, n_meta, tq_pref=2048, tk_pref=2048):
    bsz, n_heads, _, s = qT.shape
    tq = _tile(s, tq_pref)
    tk = _tile(s // 2, tk_pref)
    return pl.pallas_call(
        functools.partial(_flash_kernel, tk=tk, n_meta=n_meta),
        out_shape=jax.ShapeDtypeStruct((bsz, s, n_heads * V_DIM), BF16),
        grid=(bsz, n_heads, s // tq),
        in_specs=[pl.BlockSpec((1, 1, QK_PAD, tq), lambda b, h, i: (b, h, 0, i)),
                  pl.BlockSpec((1, 1, s, QK_PAD), lambda b, h, i: (b, h, 0, 0)),
                  pl.BlockSpec((1, 1, V_DIM, s), lambda b, h, i: (b, h, 0, 0)),
                  pl.BlockSpec((1, META_PAD, QK_PAD), lambda b, h, i: (h, 0, 0)),
                  pl.BlockSpec((1, V_DIM, META_PAD), lambda b, h, i: (h, 0, 0)),
                  pl.BlockSpec((1, tq, V_DIM), lambda b, h, i: (b, i, h))],
        out_specs=pl.BlockSpec((1, tq, V_DIM), lambda b, h, i: (b, i, h)),
        scratch_shapes=[pltpu.VMEM((V_DIM, tq), F32), pltpu.VMEM((1, tq), F32)],
        compiler_params=_params("parallel", "parallel", "arbitrary"),
        name="flash",
    )(qT, k, vT, k_meta, vT_meta, sz)


def _out_kernel(x_ref, oz_ref, u_ref, up_ref, un_ref, um_ref, bz_ref, ga_ref, gc_ref,
                cw_ref, wa_ref, wc_ref, wo_ref, fw_ref, y_ref, *, tiles_per_seq, n_meta):
    i = pl.program_id(0)
    tm = u_ref.shape[0]
    pos = i % tiles_per_seq
    u = u_ref[...].astype(F32)
    halo = up_ref.shape[0]
    prev_row = jnp.where(pos == 0,
                         um_ref[...].astype(F32)[n_meta - 1:n_meta],
                         up_ref[...].astype(F32)[halo - 1:halo])
    next_row = jnp.where(pos == tiles_per_seq - 1,
                         jnp.zeros((1, u.shape[1]), F32),
                         un_ref[...].astype(F32)[0:1])
    rid = lax.broadcasted_iota(jnp.int32, u.shape, 0)
    u_prev = jnp.where(rid == 0, prev_row, pltpu.roll(u, 1, 0))
    u_next = jnp.where(rid == tm - 1, next_row, pltpu.roll(u, tm - 1, 0))
    cw = cw_ref[...]
    conv = cw[0:1] * u_prev + cw[1:2] * u + cw[2:3] * u_next
    yc_in = (bz_ref[...].astype(F32) * conv).astype(BF16)
    y_conv = jnp.dot(yc_in, wc_ref[...], preferred_element_type=F32)
    y_attn = jnp.dot(oz_ref[...], wa_ref[...], preferred_element_type=F32)
    merged = ga_ref[...].astype(F32) * y_attn + gc_ref[...].astype(F32) * y_conv
    out = x_ref[...] + jnp.dot(merged.astype(BF16), wo_ref[...], preferred_element_type=F32)
    y_ref[...] = _rms(out, fw_ref[...])


def _out(x, oz, u, u_meta, bz, g, w, *, seq_len, n_meta, tm_pref=256):
    t, d = x.shape
    cw = u.shape[1]
    aw = oz.shape[1]
    tm = _tile(seq_len, tm_pref)
    halo = 16
    hb = tm // halo
    n_halo = t // halo
    row = lambda i: (i, 0)
    kern = functools.partial(_out_kernel, tiles_per_seq=seq_len // tm, n_meta=n_meta)
    return pl.pallas_call(
        kern,
        out_shape=jax.ShapeDtypeStruct((t, d), F32),
        grid=(t // tm,),
        in_specs=[pl.BlockSpec((tm, d), row),
                  pl.BlockSpec((tm, aw), row),
                  pl.BlockSpec((tm, cw), row),
                  pl.BlockSpec((halo, cw), lambda i: (jnp.maximum(i * hb - 1, 0), 0)),
                  pl.BlockSpec((halo, cw), lambda i: (jnp.minimum((i + 1) * hb, n_halo - 1), 0)),
                  pl.BlockSpec((halo, cw), lambda i: (0, 0)),
                  pl.BlockSpec((tm, cw), row),
                  pl.BlockSpec((tm, d), lambda i: (i, 0)),
                  pl.BlockSpec((tm, d), lambda i: (i, 1)),
                  _resident(w["conv_w"].shape),
                  _resident(w["w_o_attn"].shape),
                  _resident(w["w_o_conv"].shape),
                  _resident(w["w_o"].shape),
                  _resident((1, d))],
        out_specs=pl.BlockSpec((tm, d), row),
        compiler_params=_params("parallel"),
        name="out",
    )(x, oz, u, u, u, u_meta, bz, g, g, w["conv_w"], w["w_o_attn"], w["w_o_conv"], w["w_o"],
      w["final_norm"])


def _prep_weights(norm_w, w_in, b_gate, q_a_norm_w, w_uq, kv_a_norm_w, w_ukv,
                  w_o_attn, conv_w, w_o_conv, w_o, final_norm_w):
    d = w_in.shape[1]
    q_rank = q_a_norm_w.shape[-1]
    kv_rank = kv_a_norm_w.shape[-1]
    n_heads = w_uq.shape[-1] // QK_DIM
    aw = w_o_attn.shape[1]
    cwid = w_o_conv.shape[1]
    sizes = (q_rank, kv_rank, QK_ROPE_DIM, aw, cwid, cwid, cwid, cwid, 2 * d)
    assert sum(sizes) == w_in.shape[-1]
    cols, start = [], 0
    for sz in sizes:
        cols.append((start, start + sz))
        start += sz
    win = w_in[0]
    cut = lambda k: win[:, cols[k][0]:cols[k][1]].astype(BF16)
    wuq = w_uq[0].reshape(q_rank, n_heads, QK_DIM).transpose(1, 2, 0)
    wukv = w_ukv[0].reshape(kv_rank, n_heads, QK_NOPE_DIM + V_DIM)
    return dict(
        n_heads=n_heads, q_rank=q_rank, kv_rank=kv_rank,
        norm=norm_w[0],
        w_small=win[:, :cols[2][1]].astype(BF16),
        w_z=cut(3), w_cx=cut(4), w_cb=cut(5), w_cc=cut(6), w_zc=cut(7), w_g=cut(8),
        b_gate=b_gate[0],
        q_norm=q_a_norm_w[0].reshape(1, q_rank),
        kv_norm=kv_a_norm_w[0].reshape(1, kv_rank),
        w_uqT=wuq.reshape(n_heads * QK_DIM, q_rank).astype(BF16),
        w_uk=wukv[:, :, :QK_NOPE_DIM].reshape(kv_rank, n_heads * QK_NOPE_DIM).astype(BF16),
        w_uvT=wukv[:, :, QK_NOPE_DIM:].transpose(1, 2, 0).reshape(n_heads * V_DIM, kv_rank).astype(BF16),
        w_o_attn=w_o_attn[0].astype(BF16),
        conv_w=conv_w[0],
        w_o_conv=w_o_conv[0].astype(BF16),
        w_o=w_o[0].astype(BF16),
        final_norm=final_norm_w.reshape(1, d),
    )


def _meta_state(meta_tokens, w):
    n_meta, d = meta_tokens.shape
    xm = jnp.pad(meta_tokens, ((0, META_PAD - n_meta), (0, 0)))
    xn = _rmsnorm(xm, w["norm"])
    _, k_meta, vT_meta = _attn_prep(xn[None], w, _rope_tables(0, META_PAD))
    u_meta = _proj(xn, w["w_cx"], w["w_cc"], "mul")
    return k_meta[0], vT_meta[0], u_meta


def _trunk(x, meta, w, n_meta):
    bsz, s, d = x.shape
    k_meta, vT_meta, u_meta = meta
    x2 = x.reshape(bsz * s, d)
    xn = _rmsnorm(x2, w["norm"])
    qT, k, vT = _attn_prep(xn.reshape(bsz, s, d), w, _rope_tables(n_meta, s))
    sz = _proj(xn, w["w_z"], None, "silu")
    u = _proj(xn, w["w_cx"], w["w_cc"], "mul")
    bz = _proj(xn, w["w_cb"], w["w_zc"], "mul_silu")
    g = _proj(xn, w["w_g"], w["b_gate"], "sigmoid_bias")
    oz = _flash(qT, k, vT, k_meta, vT_meta, sz.reshape(bsz, s, -1), n_meta=n_meta)
    y = _out(x2, oz.reshape(bsz * s, -1), u, u_meta, bz, g, w, seq_len=s, n_meta=n_meta)
    return y.reshape(bsz, s, d)


def kernel(x_prompt, x_sample, meta_tokens, norm_w, w_in, b_gate, q_a_norm_w, w_uq, kv_a_norm_w, w_ukv, w_o_attn, conv_w, w_o_conv, w_o, final_norm_w):
    assert norm_w.shape[0] == 1
    n_meta = meta_tokens.shape[0]
    assert n_meta <= 16
    w = _prep_weights(norm_w, w_in, b_gate, q_a_norm_w, w_uq, kv_a_norm_w, w_ukv,
                      w_o_attn, conv_w, w_o_conv, w_o, final_norm_w)
    meta = _meta_state(meta_tokens, w)
    return (_trunk(x_prompt, meta, w, n_meta), _trunk(x_sample, meta, w, n_meta))
```

```python
import functools

import jax
import jax.numpy as jnp
from jax import lax
from jax.experimental import pallas as pl
from jax.experimental.pallas import tpu as pltpu

QK_NOPE_DIM = 128
QK_ROPE_DIM = 64
QK_DIM = QK_NOPE_DIM + QK_ROPE_DIM
V_DIM = 128
ROPE_HALF = QK_ROPE_DIM // 2
QK_PAD = 256
ROPE_THETA = 10000.0
NORM_EPS = 1e-6
MASK_VALUE = -1e30
MAX_JUMP = 64.0
LOG2_E = 1.4426950408889634
META_PAD = 128
VMEM_LIMIT_BYTES = 56 * 1024 * 1024

F32 = jnp.float32
BF16 = jnp.bfloat16


def _params(*semantics):
    return pltpu.CompilerParams(dimension_semantics=semantics,
                                vmem_limit_bytes=VMEM_LIMIT_BYTES)


def _tile(n, pref):
    if n <= pref:
        return n
    t = pref
    while n % t:
        t //= 2
    return t


def _nt_dot(a, b):
    return lax.dot_general(a, b, (((1,), (1,)), ((), ())), preferred_element_type=F32)


def _rms(x, w):
    return x * lax.rsqrt(jnp.mean(x * x, axis=-1, keepdims=True) + NORM_EPS) * w


def _resident(shape):
    nd = len(shape)
    return pl.BlockSpec(shape, lambda *_: (0,) * nd, pipeline_mode=pl.Buffered(1))


def _norm_silu_kernel(x_ref, wz_ref, nw_ref, xn_ref, sz_ref):
    x = x_ref[...]
    r = lax.rsqrt(jnp.mean(x * x, axis=-1, keepdims=True) + NORM_EPS)
    z = jnp.dot(x.astype(BF16), wz_ref[...], preferred_element_type=F32) * r
    sz_ref[...] = jax.nn.silu(z).astype(sz_ref.dtype)

    @pl.when(pl.program_id(1) == 0)
    def _():
        xn_ref[...] = (x * r * nw_ref[...]).astype(xn_ref.dtype)


def _norm_silu(x, wz_folded, norm_w, *, tm_pref=1024, tn_pref=1024):
    t, d = x.shape
    n = wz_folded.shape[1]
    tm, tn = _tile(t, tm_pref), _tile(n, tn_pref)
    return pl.pallas_call(
        _norm_silu_kernel,
        out_shape=(jax.ShapeDtypeStruct((t, d), BF16), jax.ShapeDtypeStruct((t, n), BF16)),
        grid=(t // tm, n // tn),
        in_specs=[pl.BlockSpec((tm, d), lambda i, j: (i, 0)),
                  pl.BlockSpec((d, tn), lambda i, j: (0, j)),
                  pl.BlockSpec((1, d), lambda i, j: (0, 0))],
        out_specs=(pl.BlockSpec((tm, d), lambda i, j: (i, 0)),
                   pl.BlockSpec((tm, tn), lambda i, j: (i, j))),
        compiler_params=_params("parallel", "arbitrary"),
        name="norm_silu",
    )(x, wz_folded, norm_w.reshape(1, d))


def _attn_prep_kernel(xn_ref, wsm_ref, qnw_ref, kvnw_ref, wuqT_ref, wuk_ref, wuvT_ref,
                      cosT_ref, sinT_ref, cosK_ref, sinK_ref,
                      qT_ref, k_ref, vT_ref, *, n_heads, q_rank, kv_rank, scale, sub):
    tm = xn_ref.shape[1]
    q_zero = jnp.zeros((QK_PAD - QK_DIM, sub), BF16)
    k_zero = jnp.zeros((sub, QK_PAD - QK_DIM), BF16)
    r1 = QK_NOPE_DIM + ROPE_HALF

    def group(g, carry):
        t0 = pl.multiple_of(g * sub, sub)
        tok = pl.ds(t0, sub)
        p = jnp.dot(xn_ref[0, tok, :], wsm_ref[...], preferred_element_type=F32)
        qn = _rms(p[:, :q_rank], qnw_ref[...]).astype(BF16)
        cn = _rms(p[:, q_rank:q_rank + kv_rank], kvnw_ref[...]).astype(BF16)
        kr = p[:, q_rank + kv_rank:]
        kr_sw = jnp.concatenate([kr[:, ROPE_HALF:], kr[:, :ROPE_HALF]], axis=1)
        kr_rot = (kr * cosK_ref[tok, :] + kr_sw * sinK_ref[tok, :]).astype(BF16)

        qT = _nt_dot(wuqT_ref[...], qn) * scale
        k_nope = jnp.dot(cn, wuk_ref[...], preferred_element_type=F32)
        vT = _nt_dot(wuvT_ref[...], cn)
        c = cosT_ref[:, tok]
        s = sinT_ref[:, tok]
        for h in range(n_heads):
            b = h * QK_DIM
            x1 = qT[b + QK_NOPE_DIM:b + r1]
            x2 = qT[b + r1:b + QK_DIM]
            qT_ref[0, h, 0:QK_NOPE_DIM, tok] = qT[b:b + QK_NOPE_DIM].astype(BF16)
            qT_ref[0, h, QK_NOPE_DIM:r1, tok] = (x1 * c - x2 * s).astype(BF16)
            qT_ref[0, h, r1:QK_DIM, tok] = (x2 * c + x1 * s).astype(BF16)
            qT_ref[0, h, QK_DIM:QK_PAD, tok] = q_zero
            k_ref[0, h, tok, 0:QK_NOPE_DIM] = k_nope[:, h * QK_NOPE_DIM:(h + 1) * QK_NOPE_DIM].astype(BF16)
            k_ref[0, h, tok, QK_NOPE_DIM:QK_DIM] = kr_rot
            k_ref[0, h, tok, QK_DIM:QK_PAD] = k_zero
            vT_ref[0, h, :, tok] = vT[h * V_DIM:(h + 1) * V_DIM].astype(BF16)
        return carry

    lax.fori_loop(0, tm // sub, group, 0)


def _attn_prep(xn, w, rope, *, tm_pref=512, sub_pref=256):
    bsz, s, d = xn.shape
    n_heads = w["n_heads"]
    q_rank, kv_rank = w["q_rank"], w["kv_rank"]
    tm = _tile(s, tm_pref)
    sub = _tile(tm, sub_pref)
    cosT, sinT, cosK, sinK = rope
    kern = functools.partial(_attn_prep_kernel, n_heads=n_heads, q_rank=q_rank,
                             kv_rank=kv_rank, scale=QK_DIM ** -0.5 * LOG2_E, sub=sub)
    return pl.pallas_call(
        kern,
        out_shape=(jax.ShapeDtypeStruct((bsz, n_heads, QK_PAD, s), BF16),
                   jax.ShapeDtypeStruct((bsz, n_heads, s, QK_PAD), BF16),
                   jax.ShapeDtypeStruct((bsz, n_heads, V_DIM, s), BF16)),
        grid=(bsz, s // tm),
        in_specs=[pl.BlockSpec((1, tm, d), lambda b, i: (b, i, 0)),
                  _resident(w["w_small"].shape),
                  _resident((1, q_rank)),
                  _resident((1, kv_rank)),
                  _resident(w["w_uqT"].shape),
                  _resident(w["w_uk"].shape),
                  _resident(w["w_uvT"].shape),
                  pl.BlockSpec((ROPE_HALF, tm), lambda b, i: (0, i)),
                  pl.BlockSpec((ROPE_HALF, tm), lambda b, i: (0, i)),
                  pl.BlockSpec((tm, QK_ROPE_DIM), lambda b, i: (i, 0)),
                  pl.BlockSpec((tm, QK_ROPE_DIM), lambda b, i: (i, 0))],
        out_specs=(pl.BlockSpec((1, n_heads, QK_PAD, tm), lambda b, i: (b, 0, 0, i)),
                   pl.BlockSpec((1, n_heads, tm, QK_PAD), lambda b, i: (b, 0, i, 0)),
                   pl.BlockSpec((1, n_heads, V_DIM, tm), lambda b, i: (b, 0, 0, i))),
        compiler_params=_params("parallel", "parallel"),
        name="attn_prep",
    )(xn, w["w_small"], w["q_norm"], w["kv_norm"], w["w_uqT"], w["w_uk"], w["w_uvT"],
      cosT, sinT, cosK, sinK)


def _rope_tables(start, length):
    inv_freq = 1.0 / (ROPE_THETA ** (jnp.arange(0, QK_ROPE_DIM, 2, dtype=F32) / QK_ROPE_DIM))
    pos = jnp.arange(start, start + length, dtype=F32)
    ang = pos[:, None] * inv_freq[None, :]
    cos, sin = jnp.cos(ang), jnp.sin(ang)
    cosK = jnp.concatenate([cos, cos], axis=1)
    sinK = jnp.concatenate([-sin, sin], axis=1)
    return cos.T, sin.T, cosK, sinK


def _proj_kernel(*refs, mode):
    xn_ref, w1_ref = refs[0], refs[1]
    o_ref = refs[-1]
    xn = xn_ref[...]
    p1 = jnp.dot(xn, w1_ref[...], preferred_element_type=F32)
    if mode == "sigmoid_bias":
        out = jax.nn.sigmoid(p1 + refs[2][...])
    else:
        p2 = jnp.dot(xn, refs[2][...], preferred_element_type=F32)
        out = p1 * p2 if mode == "mul" else p1 * jax.nn.silu(p2)
    o_ref[...] = out.astype(o_ref.dtype)


def _proj(xn, w1, second, mode, *, tm_pref=1024, tn_pref=1024):
    t, d = xn.shape
    n = w1.shape[1]
    single = mode == "sigmoid_bias"
    tm, tn = _tile(t, tm_pref), _tile(n, 2 * tn_pref if single else tn_pref)
    w_spec = pl.BlockSpec((d, tn), lambda i, j: (0, j))
    in_specs = [pl.BlockSpec((tm, d), lambda i, j: (i, 0)), w_spec]
    args = [xn, w1]
    if single:
        in_specs.append(pl.BlockSpec((1, tn), lambda i, j: (0, j)))
        args.append(second.reshape(1, n))
    else:
        in_specs.append(w_spec)
        args.append(second)
    return pl.pallas_call(
        functools.partial(_proj_kernel, mode=mode),
        out_shape=jax.ShapeDtypeStruct((t, n), BF16),
        grid=(t // tm, n // tn),
        in_specs=in_specs,
        out_specs=pl.BlockSpec((tm, tn), lambda i, j: (i, j)),
        compiler_params=_params("parallel", "parallel"),
        name="proj_" + mode,
    )(*args)


def _flash_kernel(qT_ref, k_ref, vT_ref, km_ref, vTm_ref, sz_ref, o_ref, acc_ref, l_ref,
                  *, tk, n_meta):
    qT = qT_ref[0, 0]
    n_kv = k_ref.shape[2] // tk

    def meta_block():
        s = jnp.dot(km_ref[0], qT, preferred_element_type=F32)
        row = lax.broadcasted_iota(jnp.int32, s.shape, 0)
        s = jnp.where(row < n_meta, s, MASK_VALUE)
        m = jnp.max(s, axis=0, keepdims=True)
        p = jnp.exp2(s - m)
        acc_ref[...] = jnp.dot(vTm_ref[0], p.astype(BF16), preferred_element_type=F32)
        return m, jnp.sum(p, axis=0, keepdims=True)

    def chunk(c, carry, *, two_pass):
        m, l, jump = carry
        off = pl.multiple_of(c * tk, tk)
        s = jnp.dot(k_ref[0, 0, pl.ds(off, tk), :], qT, preferred_element_type=F32)
        m_new = jnp.maximum(m, jnp.max(s, axis=0, keepdims=True))
        alpha = jnp.exp2(m - m_new)
        p = jnp.exp2(s - (m_new if two_pass else m))
        ls = jnp.sum(p, axis=0, keepdims=True)
        pv = jnp.dot(vT_ref[0, 0, :, pl.ds(off, tk)], p.astype(BF16),
                     preferred_element_type=F32)
        if two_pass:
            acc_ref[...] = acc_ref[...] * alpha + pv
            l = l * alpha + ls
        else:
            acc_ref[...] = (acc_ref[...] + pv) * alpha
            l = (l + ls) * alpha
        return m_new, l, jnp.maximum(jump, m_new - m)

    def sweep(*, two_pass):
        m, l = meta_block()
        carry = (m, l, jnp.zeros_like(m))
        if two_pass:
            return lax.fori_loop(0, n_kv, functools.partial(chunk, two_pass=True), carry)

        def pair(jj, carry):
            carry = chunk(2 * jj, carry, two_pass=False)
            return chunk(2 * jj + 1, carry, two_pass=False)

        n_trips = n_kv // 2 + jnp.minimum(pl.program_id(2), 0)
        return lax.fori_loop(0, n_trips, pair, carry)

    _, l, jump = sweep(two_pass=False)
    l_ref[...] = l

    @pl.when(jnp.max(jump) > MAX_JUMP)
    def _():
        _, l, _ = sweep(two_pass=True)
        l_ref[...] = l

    o = (acc_ref[...] / l_ref[...]).T
    o_ref[0] = (o * sz_ref[0].astype(F32)).astype(o_ref.dtype)


def _flash(qT, k, vT, k_meta, vT_meta, sz, *, n_meta, tq_pref=2048, tk_pref=2048):
    bsz, n_heads, _, s = qT.shape
    tq = _tile(s, tq_pref)
    tk = _tile(s // 2, tk_pref)
    return pl.pallas_call(
        functools.partial(_flash_kernel, tk=tk, n_meta=n_meta),
        out_shape=jax.ShapeDtypeStruct((bsz, s, n_heads * V_DIM), BF16),
        grid=(bsz, n_heads, s // tq),
        in_specs=[pl.BlockSpec((1, 1, QK_PAD, tq), lambda b, h, i: (b, h, 0, i)),
                  pl.BlockSpec((1, 1, s, QK_PAD), lambda b, h, i: (b, h, 0, 0)),
                  pl.BlockSpec((1, 1, V_DIM, s), lambda b, h, i: (b, h, 0, 0)),
                  pl.BlockSpec((1, META_PAD, QK_PAD), lambda b, h, i: (h, 0, 0)),
                  pl.BlockSpec((1, V_DIM, META_PAD), lambda b, h, i: (h, 0, 0)),
                  pl.BlockSpec((1, tq, V_DIM), lambda b, h, i: (b, i, h))],
        out_specs=pl.BlockSpec((1, tq, V_DIM), lambda b, h, i: (b, i, h)),
        scratch_shapes=[pltpu.VMEM((V_DIM, tq), F32), pltpu.VMEM((1, tq), F32)],
        compiler_params=_params("parallel", "parallel", "arbitrary"),
        name="flash",
    )(qT, k, vT, k_meta, vT_meta, sz)


def _out_kernel(x_ref, oz_ref, u_ref, up_ref, un_ref, um_ref, bz_ref, ga_ref, gc_ref,
                cw_ref, wa_ref, wc_ref, wo_ref, fw_ref, y_ref, *, tiles_per_seq, n_meta):
    i = pl.program_id(0)
    tm = u_ref.shape[0]
    pos = i % tiles_per_seq
    u = u_ref[...].astype(F32)
    halo = up_ref.shape[0]
    prev_row = jnp.where(pos == 0,
                         um_ref[...].astype(F32)[n_meta - 1:n_meta],
                         up_ref[...].astype(F32)[halo - 1:halo])
    next_row = jnp.where(pos == tiles_per_seq - 1,
                         jnp.zeros((1, u.shape[1]), F32),
                         un_ref[...].astype(F32)[0:1])
    rid = lax.broadcasted_iota(jnp.int32, u.shape, 0)
    u_prev = jnp.where(rid == 0, prev_row, pltpu.roll(u, 1, 0))
    u_next = jnp.where(rid == tm - 1, next_row, pltpu.roll(u, tm - 1, 0))
    cw = cw_ref[...]
    conv = cw[0:1] * u_prev + cw[1:2] * u + cw[2:3] * u_next
    yc_in = (bz_ref[...].astype(F32) * conv).astype(BF16)
    y_conv = jnp.dot(yc_in, wc_ref[...], preferred_element_type=F32)
    y_attn = jnp.dot(oz_ref[...], wa_ref[...], preferred_element_type=F32)
    merged = ga_ref[...].astype(F32) * y_attn + gc_ref[...].astype(F32) * y_conv
    out = x_ref[...] + jnp.dot(merged.astype(BF16), wo_ref[...], preferred_element_type=F32)
    y_ref[...] = _rms(out, fw_ref[...])


def _out(x, oz, u, u_meta, bz, g, w, *, seq_len, n_meta, tm_pref=256):
    t, d = x.shape
    cw = u.shape[1]
    aw = oz.shape[1]
    tm = _tile(seq_len, tm_pref)
    halo = 16
    hb = tm // halo
    n_halo = t // halo
    row = lambda i: (i, 0)
    kern = functools.partial(_out_kernel, tiles_per_seq=seq_len // tm, n_meta=n_meta)
    return pl.pallas_call(
        kern,
        out_shape=jax.ShapeDtypeStruct((t, d), F32),
        grid=(t // tm,),
        in_specs=[pl.BlockSpec((tm, d), row),
                  pl.BlockSpec((tm, aw), row),
                  pl.BlockSpec((tm, cw), row),
                  pl.BlockSpec((halo, cw), lambda i: (jnp.maximum(i * hb - 1, 0), 0)),
                  pl.BlockSpec((halo, cw), lambda i: (jnp.minimum((i + 1) * hb, n_halo - 1), 0)),
                  pl.BlockSpec((halo, cw), lambda i: (0, 0)),
                  pl.BlockSpec((tm, cw), row),
                  pl.BlockSpec((tm, d), lambda i: (i, 0)),
                  pl.BlockSpec((tm, d), lambda i: (i, 1)),
                  _resident(w["conv_w"].shape),
                  _resident(w["w_o_attn"].shape),
                  _resident(w["w_o_conv"].shape),
                  _resident(w["w_o"].shape),
                  _resident((1, d))],
        out_specs=pl.BlockSpec((tm, d), row),
        compiler_params=_params("parallel"),
        name="out",
    )(x, oz, u, u, u, u_meta, bz, g, g, w["conv_w"], w["w_o_attn"], w["w_o_conv"], w["w_o"],
      w["final_norm"])


def _prep_weights(norm_w, w_in, b_gate, q_a_norm_w, w_uq, kv_a_norm_w, w_ukv,
                  w_o_attn, conv_w, w_o_conv, w_o, final_norm_w):
    d = w_in.shape[1]
    q_rank = q_a_norm_w.shape[-1]
    kv_rank = kv_a_norm_w.shape[-1]
    n_heads = w_uq.shape[-1] // QK_DIM
    aw = w_o_attn.shape[1]
    cwid = w_o_conv.shape[1]
    sizes = (q_rank, kv_rank, QK_ROPE_DIM, aw, cwid, cwid, cwid, cwid, 2 * d)
    assert sum(sizes) == w_in.shape[-1]
    cols, start = [], 0
    for sz in sizes:
        cols.append((start, start + sz))
        start += sz
    win = w_in[0]
    cut = lambda k: win[:, cols[k][0]:cols[k][1]].astype(BF16)
    wuq = w_uq[0].reshape(q_rank, n_heads, QK_DIM).transpose(1, 2, 0)
    wukv = w_ukv[0].reshape(kv_rank, n_heads, QK_NOPE_DIM + V_DIM)
    return dict(
        n_heads=n_heads, q_rank=q_rank, kv_rank=kv_rank,
        norm=norm_w[0],
        w_small=win[:, :cols[2][1]].astype(BF16),
        w_z=(win[:, cols[3][0]:cols[3][1]] * norm_w[0][:, None]).astype(BF16),
        w_cx=cut(4), w_cb=cut(5), w_cc=cut(6), w_zc=cut(7), w_g=cut(8),
        b_gate=b_gate[0],
        q_norm=q_a_norm_w[0].reshape(1, q_rank),
        kv_norm=kv_a_norm_w[0].reshape(1, kv_rank),
        w_uqT=wuq.reshape(n_heads * QK_DIM, q_rank).astype(BF16),
        w_uk=wukv[:, :, :QK_NOPE_DIM].reshape(kv_rank, n_heads * QK_NOPE_DIM).astype(BF16),
        w_uvT=wukv[:, :, QK_NOPE_DIM:].transpose(1, 2, 0).reshape(n_heads * V_DIM, kv_rank).astype(BF16),
        w_o_attn=w_o_attn[0].astype(BF16),
        conv_w=conv_w[0],
        w_o_conv=w_o_conv[0].astype(BF16),
        w_o=w_o[0].astype(BF16),
        final_norm=final_norm_w.reshape(1, d),
    )


def _meta_state(meta_tokens, w):
    n_meta, d = meta_tokens.shape
    xm = jnp.pad(meta_tokens, ((0, META_PAD - n_meta), (0, 0)))
    xn, _ = _norm_silu(xm, w["w_z"], w["norm"])
    _, k_meta, vT_meta = _attn_prep(xn[None], w, _rope_tables(0, META_PAD))
    u_meta = _proj(xn, w["w_cx"], w["w_cc"], "mul")
    return k_meta[0], vT_meta[0], u_meta


def _trunk(x, meta, w, n_meta):
    bsz, s, d = x.shape
    k_meta, vT_meta, u_meta = meta
    x2 = x.reshape(bsz * s, d)
    xn, sz = _norm_silu(x2, w["w_z"], w["norm"])
    qT, k, vT = _attn_prep(xn.reshape(bsz, s, d), w, _rope_tables(n_meta, s))
    u =_proj(xn, w["w_cx"], w["w_cc"], "mul")
    bz = _proj(xn, w["w_cb"], w["w_zc"], "mul_silu")
    g = _proj(xn, w["w_g"], w["b_gate"], "sigmoid_bias")
    oz = _flash(qT, k, vT, k_meta, vT_meta, sz.reshape(bsz, s, -1), n_meta=n_meta)
    y = _out(x2, oz.reshape(bsz * s, -1), u, u_meta, bz, g, w, seq_len=s, n_meta=n_meta)
    return y.reshape(bsz, s, d)


def kernel(x_prompt, x_sample, meta_tokens, norm_w, w_in, b_gate, q_a_norm_w, w_uq, kv_a_norm_w, w_ukv, w_o_attn, conv_w, w_o_conv, w_o, final_norm_w):
    assert norm_w.shape[0] == 1
    n_meta = meta_tokens.shape[0]
    assert n_meta <= 16
    w = _prep_weights(norm_w, w_in, b_gate, q_a_norm_w, w_uq, kv_a_norm_w, w_ukv,
                      w_o_attn, conv_w, w_o_conv, w_o, final_norm_w)
    meta = _meta_state(meta_tokens, w)
    return (_trunk(x_prompt, meta, w, n_meta), _trunk(x_sample, meta, w, n_meta))
```

```python
import functools

import jax
import jax.numpy as jnp
from jax import lax
from jax.experimental import pallas as pl
from jax.experimental.pallas import tpu as pltpu

QK_NOPE_DIM = 128
QK_ROPE_DIM = 64
QK_DIM = QK_NOPE_DIM + QK_ROPE_DIM
V_DIM = 128
ROPE_HALF = QK_ROPE_DIM // 2
QK_PAD = 256
ROPE_THETA = 10000.0
NORM_EPS = 1e-6
MASK_VALUE = -1e30
MAX_JUMP = 64.0
LOG2_E = 1.4426950408889634
META_PAD = 128
VMEM_LIMIT_BYTES = 56 * 1024 * 1024

F32 = jnp.float32
BF16 = jnp.bfloat16


def _params(*semantics):
    return pltpu.CompilerParams(dimension_semantics=semantics,
                                vmem_limit_bytes=VMEM_LIMIT_BYTES)


def _tile(n, pref):
    if n <= pref:
        return n
    t = pref
    while n % t:
        t //= 2
    return t


def _nt_dot(a, b):
    return lax.dot_general(a, b, (((1,), (1,)), ((), ())), preferred_element_type=F32)


def _rms(x, w):
    return x * lax.rsqrt(jnp.mean(x * x, axis=-1, keepdims=True) + NORM_EPS) * w


def _resident(shape):
    nd = len(shape)
    return pl.BlockSpec(shape, lambda *_: (0,) * nd, pipeline_mode=pl.Buffered(1))


def _norm_silu_kernel(x_ref, wz_ref, nw_ref, xn_ref, sz_ref):
    x = x_ref[...]
    r = lax.rsqrt(jnp.mean(x * x, axis=-1, keepdims=True) + NORM_EPS)
    z = _nt_dot(x.astype(BF16), wz_ref[...]) * r
    sz_ref[...] = jax.nn.silu(z).astype(sz_ref.dtype)

    @pl.when(pl.program_id(1) == 0)
    def _():
        xn_ref[...] = (x * r * nw_ref[...]).astype(xn_ref.dtype)


def _norm_silu(x, wz_folded, norm_w, *, tm_pref=1024, tn_pref=1024):
    t, d = x.shape
    n = wz_folded.shape[0]
    tm, tn = _tile(t, tm_pref), _tile(n, tn_pref)
    return pl.pallas_call(
        _norm_silu_kernel,
        out_shape=(jax.ShapeDtypeStruct((t, d), BF16), jax.ShapeDtypeStruct((t, n), BF16)),
        grid=(t // tm, n // tn),
        in_specs=[pl.BlockSpec((tm, d), lambda i, j: (i, 0)),
                  pl.BlockSpec((tn, d), lambda i, j: (j, 0)),
                  pl.BlockSpec((1, d), lambda i, j: (0, 0))],
        out_specs=(pl.BlockSpec((tm, d), lambda i, j: (i, 0)),
                   pl.BlockSpec((tm, tn), lambda i, j: (i, j))),
        compiler_params=_params("parallel", "arbitrary"),
        name="norm_silu",
    )(x, wz_folded, norm_w.reshape(1, d))


def _attn_prep_kernel(xn_ref, wsm_ref, qnw_ref, kvnw_ref, wuqT_ref, wuk_ref, wuvT_ref,
                      cosT_ref, sinT_ref, cosK_ref, sinK_ref,
                      qT_ref, k_ref, vT_ref, *, n_heads, q_rank, kv_rank, scale, sub):
    tm = xn_ref.shape[1]
    q_zero = jnp.zeros((QK_PAD - QK_DIM, sub), BF16)
    k_zero = jnp.zeros((sub, QK_PAD - QK_DIM), BF16)
    r1 = QK_NOPE_DIM + ROPE_HALF

    def group(g, carry):
        t0 = pl.multiple_of(g * sub, sub)
        tok = pl.ds(t0, sub)
        p = jnp.dot(xn_ref[0, tok, :], wsm_ref[...], preferred_element_type=F32)
        qn = _rms(p[:, :q_rank], qnw_ref[...]).astype(BF16)
        cn = _rms(p[:, q_rank:q_rank + kv_rank], kvnw_ref[...]).astype(BF16)
        kr = p[:, q_rank + kv_rank:]
        kr_sw = jnp.concatenate([kr[:, ROPE_HALF:], kr[:, :ROPE_HALF]], axis=1)
        kr_rot = (kr * cosK_ref[tok, :] + kr_sw * sinK_ref[tok, :]).astype(BF16)

        qT = _nt_dot(wuqT_ref[...], qn) * scale
        k_nope = jnp.dot(cn, wuk_ref[...], preferred_element_type=F32)
        vT = _nt_dot(wuvT_ref[...], cn)
        c = cosT_ref[:, tok]
        s = sinT_ref[:, tok]
        for h in range(n_heads):
            b = h * QK_DIM
            x1 = qT[b + QK_NOPE_DIM:b + r1]
            x2 = qT[b + r1:b + QK_DIM]
            qT_ref[0, h, 0:QK_NOPE_DIM, tok] = qT[b:b + QK_NOPE_DIM].astype(BF16)
            qT_ref[0, h, QK_NOPE_DIM:r1, tok] = (x1 * c - x2 * s).astype(BF16)
            qT_ref[0, h, r1:QK_DIM, tok] = (x2 * c + x1 * s).astype(BF16)
            qT_ref[0, h, QK_DIM:QK_PAD, tok] = q_zero
            k_ref[0, h, tok, 0:QK_NOPE_DIM] = k_nope[:, h * QK_NOPE_DIM:(h + 1) * QK_NOPE_DIM].astype(BF16)
            k_ref[0, h, tok, QK_NOPE_DIM:QK_DIM] = kr_rot
            k_ref[0, h, tok, QK_DIM:QK_PAD] = k_zero
            vT_ref[0, h, :, tok] = vT[h * V_DIM:(h + 1) * V_DIM].astype(BF16)
        return carry

    lax.fori_loop(0, tm // sub, group, 0)


def _attn_prep(xn, w, rope, *, tm_pref=512, sub_pref=256):
    bsz, s, d = xn.shape
    n_heads = w["n_heads"]
    q_rank, kv_rank = w["q_rank"], w["kv_rank"]
    tm = _tile(s, tm_pref)
    sub = _tile(tm, sub_pref)
    cosT, sinT, cosK, sinK = rope
    kern = functools.partial(_attn_prep_kernel, n_heads=n_heads, q_rank=q_rank,
                             kv_rank=kv_rank, scale=QK_DIM ** -0.5 * LOG2_E, sub=sub)
    return pl.pallas_call(
        kern,
        out_shape=(jax.ShapeDtypeStruct((bsz, n_heads, QK_PAD, s), BF16),
                   jax.ShapeDtypeStruct((bsz, n_heads, s, QK_PAD), BF16),
                   jax.ShapeDtypeStruct((bsz, n_heads, V_DIM, s), BF16)),
        grid=(bsz, s // tm),
        in_specs=[pl.BlockSpec((1, tm, d), lambda b, i: (b, i, 0)),
                  _resident(w["w_small"].shape),
                  _resident((1, q_rank)),
                  _resident((1, kv_rank)),
                  _resident(w["w_uqT"].shape),
                  _resident(w["w_uk"].shape),
                  _resident(w["w_uvT"].shape),
                  pl.BlockSpec((ROPE_HALF, tm), lambda b, i: (0, i)),
                  pl.BlockSpec((ROPE_HALF, tm), lambda b, i: (0, i)),
                  pl.BlockSpec((tm, QK_ROPE_DIM), lambda b, i: (i, 0)),
                  pl.BlockSpec((tm, QK_ROPE_DIM), lambda b, i: (i, 0))],
        out_specs=(pl.BlockSpec((1, n_heads, QK_PAD, tm), lambda b, i: (b, 0, 0, i)),
                   pl.BlockSpec((1, n_heads, tm, QK_PAD), lambda b, i: (b, 0, i, 0)),
                   pl.BlockSpec((1, n_heads, V_DIM, tm), lambda b, i: (b, 0, 0, i))),
        compiler_params=_params("parallel", "parallel"),
        name="attn_prep",
    )(xn, w["w_small"], w["q_norm"], w["kv_norm"], w["w_uqT"], w["w_uk"], w["w_uvT"],
      cosT, sinT, cosK, sinK)


def _rope_tables(start, length):
    inv_freq = 1.0 / (ROPE_THETA ** (jnp.arange(0, QK_ROPE_DIM, 2, dtype=F32) / QK_ROPE_DIM))
    pos = jnp.arange(start, start + length, dtype=F32)
    ang = pos[:, None] * inv_freq[None, :]
    cos, sin = jnp.cos(ang), jnp.sin(ang)
    cosK = jnp.concatenate([cos, cos], axis=1)
    sinK = jnp.concatenate([-sin, sin], axis=1)
    return cos.T, sin.T, cosK, sinK


def _proj_kernel(*refs, mode):
    xn_ref, w1_ref = refs[0], refs[1]
    o_ref = refs[-1]
    xn = xn_ref[...]
    p1 = _nt_dot(xn, w1_ref[...])
    if mode == "sigmoid_bias":
        out = jax.nn.sigmoid(p1 + refs[2][...])
    else:
        p2 = _nt_dot(xn, refs[2][...])
        out = p1 * p2 if mode == "mul" else p1 * jax.nn.silu(p2)
    o_ref[...] = out.astype(o_ref.dtype)


def _proj(xn, w1, second, mode, *, tm_pref=1024, tn_pref=1024):
    t, d = xn.shape
    n = w1.shape[0]
    single = mode == "sigmoid_bias"
    tm, tn = _tile(t, tm_pref), _tile(n, 2 * tn_pref if single else tn_pref)
    w_spec = pl.BlockSpec((tn, d), lambda i, j: (j, 0))
    in_specs = [pl.BlockSpec((tm, d), lambda i, j: (i, 0)), w_spec]
    args = [xn, w1]
    if single:
        in_specs.append(pl.BlockSpec((1, tn), lambda i, j: (0, j)))
        args.append(second.reshape(1, n))
    else:
        in_specs.append(w_spec)
        args.append(second)
    return pl.pallas_call(
        functools.partial(_proj_kernel, mode=mode),
        out_shape=jax.ShapeDtypeStruct((t, n), BF16),
        grid=(t // tm, n // tn),
        in_specs=in_specs,
        out_specs=pl.BlockSpec((tm, tn), lambda i, j: (i, j)),
        compiler_params=_params("parallel", "parallel"),
        name="proj_" + mode,
    )(*args)


def _flash_kernel(qT_ref, k_ref, vT_ref, km_ref, vTm_ref, sz_ref, o_ref, acc_ref, l_ref,
                  *, tk, n_meta):
    qT = qT_ref[0, 0]
    n_kv = k_ref.shape[2] // tk

    def meta_block():
        s = jnp.dot(km_ref[0], qT, preferred_element_type=F32)
        row = lax.broadcasted_iota(jnp.int32, s.shape, 0)
        s = jnp.where(row < n_meta, s, MASK_VALUE)
        m = jnp.max(s, axis=0, keepdims=True)
        p = jnp.exp2(s - m)
        acc_ref[...] = jnp.dot(vTm_ref[0], p.astype(BF16), preferred_element_type=F32)
        return m, jnp.sum(p, axis=0, keepdims=True)

    def chunk(c, carry, *, two_pass):
        m, l, jump = carry
        off = pl.multiple_of(c * tk, tk)
        s = jnp.dot(k_ref[0, 0, pl.ds(off, tk), :], qT, preferred_element_type=F32)
        m_new = jnp.maximum(m, jnp.max(s, axis=0, keepdims=True))
        alpha = jnp.exp2(m - m_new)
        p = jnp.exp2(s - (m_new if two_pass else m))
        ls = jnp.sum(p, axis=0, keepdims=True)
        pv = jnp.dot(vT_ref[0, 0, :, pl.ds(off, tk)], p.astype(BF16),
                     preferred_element_type=F32)
        if two_pass:
            acc_ref[...] = acc_ref[...] * alpha + pv
            l = l * alpha + ls
        else:
            acc_ref[...] = (acc_ref[...] + pv) * alpha
            l = (l + ls) * alpha
        return m_new, l, jnp.maximum(jump, m_new - m)

    def sweep(*, two_pass):
        m, l = meta_block()
        carry = (m, l, jnp.zeros_like(m))
        if two_pass:
            return lax.fori_loop(0, n_kv, functools.partial(chunk, two_pass=True), carry)

        def pair(jj, carry):
            carry = chunk(2 * jj, carry, two_pass=False)
            return chunk(2 * jj + 1, carry, two_pass=False)

        n_trips = n_kv // 2 + jnp.minimum(pl.program_id(2), 0)
        return lax.fori_loop(0, n_trips, pair, carry)

    _, l, jump = sweep(two_pass=False)
    l_ref[...] = l

    @pl.when(jnp.max(jump) > MAX_JUMP)
    def _():
        _, l, _ = sweep(two_pass=True)
        l_ref[...] = l

    o = (acc_ref[...] / l_ref[...]).T
    o_ref[0] = (o * sz_ref[0].astype(F32)).astype(o_ref.dtype)


def _flash(qT, k, vT, k_meta, vT_meta, sz, *, n_meta, tq_pref=2048, tk_pref=2048):
    bsz, n_heads, _, s = qT.shape
    tq = _tile(s, tq_pref)
    tk = _tile(s // 2, tk_pref)
    return pl.pallas_call(
        functools.partial(_flash_kernel, tk=tk, n_meta=n_meta),
        out_shape=jax.ShapeDtypeStruct((bsz, s, n_heads * V_DIM), BF16),
        grid=(bsz, n_heads, s // tq),
        in_specs=[pl.BlockSpec((1, 1, QK_PAD, tq), lambda b, h, i: (b, h, 0, i)),
                  pl.BlockSpec((1, 1, s, QK_PAD), lambda b, h, i: (b, h, 0, 0)),
                  pl.BlockSpec((1, 1, V_DIM, s), lambda b, h, i: (b, h, 0, 0)),
                  pl.BlockSpec((1, META_PAD, QK_PAD), lambda b, h, i: (h, 0, 0)),
                  pl.BlockSpec((1, V_DIM, META_PAD), lambda b, h, i: (h, 0, 0)),
                  pl.BlockSpec((1, tq, V_DIM), lambda b, h, i: (b, i, h))],
        out_specs=pl.BlockSpec((1, tq, V_DIM), lambda b, h, i: (b, i, h)),
        scratch_shapes=[pltpu.VMEM((V_DIM, tq), F32), pltpu.VMEM((1, tq), F32)],
        compiler_params=_params("parallel", "parallel", "arbitrary"),
        name="flash",
    )(qT, k, vT, k_meta, vT_meta, sz)


def _out_kernel(x_ref, oz_ref, u_ref, up_ref, un_ref, um_ref, bz_ref, ga_ref, gc_ref,
                cw_ref, wa_ref, wc_ref, wo_ref, fw_ref, y_ref, *, tiles_per_seq, n_meta):
    i = pl.program_id(0)
    tm = u_ref.shape[0]
    pos = i % tiles_per_seq
    u = u_ref[...].astype(F32)
    halo = up_ref.shape[0]
    prev_row = jnp.where(pos == 0,
                         um_ref[...].astype(F32)[n_meta - 1:n_meta],
                         up_ref[...].astype(F32)[halo - 1:halo])
    next_row = jnp.where(pos == tiles_per_seq - 1,
                         jnp.zeros((1, u.shape[1]), F32),
                         un_ref[...].astype(F32)[0:1])
    rid = lax.broadcasted_iota(jnp.int32, u.shape, 0)
    u_prev = jnp.where(rid == 0, prev_row, pltpu.roll(u, 1, 0))
    u_next = jnp.where(rid == tm - 1, next_row, pltpu.roll(u, tm - 1, 0))
    cw = cw_ref[...]
    conv = cw[0:1] * u_prev + cw[1:2] * u + cw[2:3] * u_next
    yc_in = (bz_ref[...].astype(F32) * conv).astype(BF16)
    y_conv = jnp.dot(yc_in, wc_ref[...], preferred_element_type=F32)
    y_attn = jnp.dot(oz_ref[...], wa_ref[...], preferred_element_type=F32)
    merged = ga_ref[...].astype(F32) * y_attn + gc_ref[...].astype(F32) * y_conv
    out = x_ref[...] + jnp.dot(merged.astype(BF16), wo_ref[...], preferred_element_type=F32)
    y_ref[...] = _rms(out, fw_ref[...])


def _out(x, oz, u, u_meta, bz, g, w, *, seq_len, n_meta, tm_pref=256):
    t, d = x.shape
    cw = u.shape[1]
    aw = oz.shape[1]
    tm = _tile(seq_len, tm_pref)
    halo = 16
    hb = tm // halo
    n_halo = t // halo
    row = lambda i: (i, 0)
    kern = functools.partial(_out_kernel, tiles_per_seq=seq_len // tm, n_meta=n_meta)
    return pl.pallas_call(
        kern,
        out_shape=jax.ShapeDtypeStruct((t, d), F32),
        grid=(t // tm,),
        in_specs=[pl.BlockSpec((tm, d), row),
                  pl.BlockSpec((tm, aw), row),
                  pl.BlockSpec((tm, cw), row),
                  pl.BlockSpec((halo, cw), lambda i: (jnp.maximum(i * hb - 1, 0), 0)),
                  pl.BlockSpec((halo, cw), lambda i: (jnp.minimum((i + 1) * hb, n_halo - 1), 0)),
                  pl.BlockSpec((halo, cw), lambda i: (0, 0)),
                  pl.BlockSpec((tm, cw), row),
                  pl.BlockSpec((tm, d), lambda i: (i, 0)),
                  pl.BlockSpec((tm, d), lambda i: (i, 1)),
                  _resident(w["conv_w"].shape),
                  _resident(w["w_o_attn"].shape),
                  _resident(w["w_o_conv"].shape),
                  _resident(w["w_o"].shape),
                  _resident((1, d))],
        out_specs=pl.BlockSpec((tm, d), row),
        compiler_params=_params("parallel"),
        name="out",
    )(x, oz, u, u, u, u_meta, bz, g, g, w["conv_w"], w["w_o_attn"], w["w_o_conv"], w["w_o"],
      w["final_norm"])


def _prep_weights(norm_w, w_in, b_gate, q_a_norm_w, w_uq, kv_a_norm_w, w_ukv,
                  w_o_attn, conv_w, w_o_conv, w_o, final_norm_w):
    d = w_in.shape[1]
    q_rank = q_a_norm_w.shape[-1]
    kv_rank = kv_a_norm_w.shape[-1]
    n_heads = w_uq.shape[-1] // QK_DIM
    aw = w_o_attn.shape[1]
    cwid = w_o_conv.shape[1]
    sizes = (q_rank, kv_rank, QK_ROPE_DIM, aw, cwid, cwid, cwid, cwid, 2 * d)
    assert sum(sizes) == w_in.shape[-1]
    cols, start = [], 0
    for sz in sizes:
        cols.append((start, start + sz))
        start += sz
    wt = jnp.transpose(w_in[0])
    cut = lambda k: wt[cols[k][0]:cols[k][1]].astype(BF16)
    wuq = w_uq[0].reshape(q_rank, n_heads, QK_DIM).transpose(1, 2, 0)
    wukv = w_ukv[0].reshape(kv_rank, n_heads, QK_NOPE_DIM + V_DIM)
    return dict(
        n_heads=n_heads, q_rank=q_rank, kv_rank=kv_rank,
        norm=norm_w[0],
        w_small=jnp.transpose(wt[:cols[2][1]].astype(BF16)),
        w_z=(wt[cols[3][0]:cols[3][1]] * norm_w[0][None, :]).astype(BF16),
        w_cx=cut(4), w_cb=cut(5), w_cc=cut(6), w_zc=cut(7), w_g=cut(8),
        b_gate=b_gate[0],
        q_norm=q_a_norm_w[0].reshape(1, q_rank),
        kv_norm=kv_a_norm_w[0].reshape(1, kv_rank),
        w_uqT=wuq.reshape(n_heads * QK_DIM, q_rank).astype(BF16),
        w_uk=wukv[:, :, :QK_NOPE_DIM].reshape(kv_rank, n_heads * QK_NOPE_DIM).astype(BF16),
        w_uvT=wukv[:, :, QK_NOPE_DIM:].transpose(1, 2, 0).reshape(n_heads * V_DIM, kv_rank).astype(BF16),
        w_o_attn=w_o_attn[0].astype(BF16),
        conv_w=conv_w[0],
        w_o_conv=w_o_conv[0].astype(BF16),
        w_o=w_o[0].astype(BF16),
        final_norm=final_norm_w.reshape(1, d),
    )


def _meta_state(meta_tokens, w):
    n_meta, d = meta_tokens.shape
    xm = jnp.pad(meta_tokens, ((0, META_PAD - n_meta), (0, 0)))
    xn, _ = _norm_silu(xm, w["w_z"], w["norm"])
    _, k_meta, vT_meta = _attn_prep(xn[None], w, _rope_tables(0, META_PAD))
    u_meta = _proj(xn, w["w_cx"], w["w_cc"], "mul")
    return k_meta[0], vT_meta[0], u_meta


def _trunk(x, meta, w, n_meta):
    bsz, s, d = x.shape
    k_meta, vT_meta, u_meta = meta
    x2 = x.reshape(bsz * s, d)
    xn, sz = _norm_silu(x2, w["w_z"], w["norm"])
    qT, k, vT = _attn_prep(xn.reshape(bsz, s, d), w, _rope_tables(n_meta, s))
    u =_proj(xn, w["w_cx"], w["w_cc"], "mul")
    bz = _proj(xn, w["w_cb"], w["w_zc"], "mul_silu")
    g = _proj(xn, w["w_g"], w["b_gate"], "sigmoid_bias")
    oz = _flash(qT, k, vT, k_meta, vT_meta, sz.reshape(bsz, s, -1), n_meta=n_meta)
    y = _out(x2, oz.reshape(bsz * s, -1), u, u_meta, bz, g, w, seq_len=s, n_meta=n_meta)
    return y.reshape(bsz, s, d)


def kernel(x_prompt, x_sample, meta_tokens, norm_w, w_in, b_gate, q_a_norm_w, w_uq, kv_a_norm_w, w_ukv, w_o_attn, conv_w, w_o_conv, w_o, final_norm_w):
    assert norm_w.shape[0] == 1
    n_meta = meta_tokens.shape[0]
    assert n_meta <= 16
    w = _prep_weights(norm_w, w_in, b_gate, q_a_norm_w, w_uq, kv_a_norm_w, w_ukv,
                      w_o_attn, conv_w, w_o_conv, w_o, final_norm_w)
    meta = _meta_state(meta_tokens, w)
    return (_trunk(x_prompt, meta, w, n_meta), _trunk(x_sample, meta, w, n_meta))
```

```python
import functools

import jax
import jax.numpy as jnp
from jax import lax
from jax.experimental import pallas as pl
from jax.experimental.pallas import tpu as pltpu

QK_NOPE_DIM = 128
QK_ROPE_DIM = 64
QK_DIM = QK_NOPE_DIM + QK_ROPE_DIM
V_DIM = 128
ROPE_HALF = QK_ROPE_DIM // 2
QK_PAD = 256
ROPE_THETA = 10000.0
NORM_EPS = 1e-6
MASK_VALUE = -1e30
MAX_JUMP = 64.0
LOG2_E = 1.4426950408889634
META_PAD = 128
VMEM_LIMIT_BYTES = 56 * 1024 * 1024

F32 = jnp.float32
BF16 = jnp.bfloat16


def _params(*semantics):
    return pltpu.CompilerParams(dimension_semantics=semantics,
                                vmem_limit_bytes=VMEM_LIMIT_BYTES)


def _tile(n, pref):
    if n <= pref:
        return n
    t = pref
    while n % t:
        t //= 2
    return t


def _nt_dot(a, b):
    return lax.dot_general(a, b, (((1,), (1,)), ((), ())), preferred_element_type=F32)


def _rms(x, w):
    return x * lax.rsqrt(jnp.mean(x * x, axis=-1, keepdims=True) + NORM_EPS) * w


def _resident(shape):
    nd = len(shape)
    return pl.BlockSpec(shape, lambda *_: (0,) * nd, pipeline_mode=pl.Buffered(1))


def _norm_silu_kernel(x_ref, wz_ref, nw_ref, xn_ref, sz_ref):
    x = x_ref[...]
    r = lax.rsqrt(jnp.mean(x * x, axis=-1, keepdims=True) + NORM_EPS)
    z = _nt_dot(x.astype(BF16), wz_ref[...]) * r
    sz_ref[...] = jax.nn.silu(z).astype(sz_ref.dtype)

    @pl.when(pl.program_id(1) == 0)
    def _():
        xn_ref[...] = (x * r * nw_ref[...]).astype(xn_ref.dtype)


def _norm_silu(x, wz_folded, norm_w, *, tm_pref=1024, tn_pref=1024):
    t, d = x.shape
    n = wz_folded.shape[0]
    tm, tn = _tile(t, tm_pref), _tile(n, tn_pref)
    return pl.pallas_call(
        _norm_silu_kernel,
        out_shape=(jax.ShapeDtypeStruct((t, d), BF16), jax.ShapeDtypeStruct((t, n), BF16)),
        grid=(t // tm, n // tn),
        in_specs=[pl.BlockSpec((tm, d), lambda i, j: (i, 0)),
                  pl.BlockSpec((tn, d), lambda i, j: (j, 0)),
                  pl.BlockSpec((1, d), lambda i, j: (0, 0))],
        out_specs=(pl.BlockSpec((tm, d), lambda i, j: (i, 0)),
                   pl.BlockSpec((tm, tn), lambda i, j: (i, j))),
        compiler_params=_params("parallel", "arbitrary"),
        name="norm_silu",
    )(x, wz_folded, norm_w.reshape(1, d))


def _attn_prep_kernel(xn_ref, wsm_ref, qnw_ref, kvnw_ref, wuqT_ref, wuk_ref, wuvT_ref,
                      cosT_ref, sinT_ref, cosK_ref, sinK_ref,
                      qT_ref, k_ref, vT_ref, *, n_heads, q_rank, kv_rank, scale, sub):
    tm = xn_ref.shape[1]
    q_zero = jnp.zeros((QK_PAD - QK_DIM, sub), BF16)
    k_zero = jnp.zeros((sub, QK_PAD - QK_DIM), BF16)
    r1 = QK_NOPE_DIM + ROPE_HALF

    def group(g, carry):
        t0 = pl.multiple_of(g * sub, sub)
        tok = pl.ds(t0, sub)
        p = jnp.dot(xn_ref[0, tok, :], wsm_ref[...], preferred_element_type=F32)
        qn = _rms(p[:, :q_rank], qnw_ref[...]).astype(BF16)
        cn = _rms(p[:, q_rank:q_rank + kv_rank], kvnw_ref[...]).astype(BF16)
        kr = p[:, q_rank + kv_rank:]
        kr_sw = jnp.concatenate([kr[:, ROPE_HALF:], kr[:, :ROPE_HALF]], axis=1)
        kr_rot = (kr * cosK_ref[tok, :] + kr_sw * sinK_ref[tok, :]).astype(BF16)

        qT = _nt_dot(wuqT_ref[...], qn) * scale
        k_nope = jnp.dot(cn, wuk_ref[...], preferred_element_type=F32)
        vT = _nt_dot(wuvT_ref[...], cn)
        c = cosT_ref[:, tok]
        s = sinT_ref[:, tok]
        for h in range(n_heads):
            b = h * QK_DIM
            x1 = qT[b + QK_NOPE_DIM:b + r1]
            x2 = qT[b + r1:b + QK_DIM]
            qT_ref[0, h, 0, 0:QK_NOPE_DIM, tok] = qT[b:b + QK_NOPE_DIM].astype(BF16)
            qT_ref[0, h, 0, QK_NOPE_DIM:r1, tok] = (x1 * c - x2 * s).astype(BF16)
            qT_ref[0, h, 0, r1:QK_DIM, tok] = (x2 * c + x1 * s).astype(BF16)
            qT_ref[0, h, 0, QK_DIM:QK_PAD, tok] = q_zero
            k_ref[0, h, tok, 0:QK_NOPE_DIM] = k_nope[:, h * QK_NOPE_DIM:(h + 1) * QK_NOPE_DIM].astype(BF16)
            k_ref[0, h, tok, QK_NOPE_DIM:QK_DIM] = kr_rot
            k_ref[0, h, tok, QK_DIM:QK_PAD] = k_zero
            vT_ref[0, h, 0, :, tok] = vT[h * V_DIM:(h + 1) * V_DIM].astype(BF16)
        return carry

    lax.fori_loop(0, tm // sub, group, 0)


def _attn_prep(xn, w, rope, *, tm_pref=512, sub_pref=256):
    bsz, s, d = xn.shape
    n_heads = w["n_heads"]
    q_rank, kv_rank = w["q_rank"], w["kv_rank"]
    tm = _tile(s, tm_pref)
    sub = _tile(tm, sub_pref)
    cosT, sinT, cosK, sinK = rope
    kern = functools.partial(_attn_prep_kernel, n_heads=n_heads, q_rank=q_rank,
                             kv_rank=kv_rank, scale=QK_DIM ** -0.5 * LOG2_E, sub=sub)
    return pl.pallas_call(
        kern,
        out_shape=(jax.ShapeDtypeStruct((bsz, n_heads, s // tm, QK_PAD, tm), BF16),
                   jax.ShapeDtypeStruct((bsz, n_heads, s, QK_PAD), BF16),
                   jax.ShapeDtypeStruct((bsz, n_heads, s // tm, V_DIM, tm), BF16)),
        grid=(bsz, s // tm),
        in_specs=[pl.BlockSpec((1, tm, d), lambda b, i: (b, i, 0)),
                  _resident(w["w_small"].shape),
                  _resident((1, q_rank)),
                  _resident((1, kv_rank)),
                  _resident(w["w_uqT"].shape),
                  _resident(w["w_uk"].shape),
                  _resident(w["w_uvT"].shape),
                  pl.BlockSpec((ROPE_HALF, tm), lambda b, i: (0, i)),
                  pl.BlockSpec((ROPE_HALF, tm), lambda b, i: (0, i)),
                  pl.BlockSpec((tm, QK_ROPE_DIM), lambda b, i: (i, 0)),
                  pl.BlockSpec((tm, QK_ROPE_DIM), lambda b, i: (i, 0))],
        out_specs=(pl.BlockSpec((1, n_heads, 1, QK_PAD, tm), lambda b, i: (b, 0, i, 0, 0)),
                   pl.BlockSpec((1, n_heads, tm, QK_PAD), lambda b, i: (b, 0, i, 0)),
                   pl.BlockSpec((1, n_heads, 1, V_DIM, tm), lambda b, i: (b, 0, i, 0, 0))),
        compiler_params=_params("parallel", "parallel"),
        name="attn_prep",
    )(xn, w["w_small"], w["q_norm"], w["kv_norm"], w["w_uqT"], w["w_uk"], w["w_uvT"],
      cosT, sinT, cosK, sinK)


def _rope_tables(start, length):
    inv_freq = 1.0 / (ROPE_THETA ** (jnp.arange(0, QK_ROPE_DIM, 2, dtype=F32) / QK_ROPE_DIM))
    pos = jnp.arange(start, start + length, dtype=F32)
    ang = pos[:, None] * inv_freq[None, :]
    cos, sin = jnp.cos(ang), jnp.sin(ang)
    cosK = jnp.concatenate([cos, cos], axis=1)
    sinK = jnp.concatenate([-sin, sin], axis=1)
    return cos.T, sin.T, cosK, sinK


def _proj_kernel(*refs, mode):
    xn_ref, w1_ref = refs[0], refs[1]
    o_ref = refs[-1]
    xn = xn_ref[...]
    p1 = _nt_dot(xn, w1_ref[...])
    if mode == "sigmoid_bias":
        out = jax.nn.sigmoid(p1 + refs[2][...])
    else:
        p2 = _nt_dot(xn, refs[2][...])
        out = p1 * p2 if mode == "mul" else p1 * jax.nn.silu(p2)
    o_ref[...] = out.astype(o_ref.dtype)


def _proj(xn, w1, second, mode, *, tm_pref=1024, tn_pref=1024):
    t, d = xn.shape
    n = w1.shape[0]
    single = mode == "sigmoid_bias"
    tm, tn = _tile(t, tm_pref), _tile(n, 2 * tn_pref if single else tn_pref)
    w_spec = pl.BlockSpec((tn, d), lambda i, j: (j, 0))
    in_specs = [pl.BlockSpec((tm, d), lambda i, j: (i, 0)), w_spec]
    args = [xn, w1]
    if single:
        in_specs.append(pl.BlockSpec((1, tn), lambda i, j: (0, j)))
        args.append(second.reshape(1, n))
    else:
        in_specs.append(w_spec)
        args.append(second)
    return pl.pallas_call(
        functools.partial(_proj_kernel, mode=mode),
        out_shape=jax.ShapeDtypeStruct((t, n), BF16),
        grid=(t // tm, n // tn),
        in_specs=in_specs,
        out_specs=pl.BlockSpec((tm, tn), lambda i, j: (i, j)),
        compiler_params=_params("parallel", "parallel"),
        name="proj_" + mode,
    )(*args)


def _flash_kernel(qT_ref, k_ref, vT_ref, km_ref, vTm_ref, sz_ref, o_ref, acc_ref, l_ref,
                  *, tk, n_meta):
    n_qb, _, qb = qT_ref.shape[2:]
    n_kv = k_ref.shape[2] // tk
    kb_per_chunk = tk // vT_ref.shape[-1]

    def scores(k):
        return jnp.concatenate(
            [jnp.dot(k, qT_ref[0, 0, g], preferred_element_type=F32) for g in range(n_qb)], axis=1)

    def weighted_values(first_block, p):
        vb = vT_ref.shape[-1]
        return sum(jnp.dot(vT_ref[0, 0, first_block + j], p[j * vb:(j + 1) * vb],
                           preferred_element_type=F32) for j in range(p.shape[0] // vb))

    def meta_block():
        s = scores(km_ref[0])
        row = lax.broadcasted_iota(jnp.int32, s.shape, 0)
        s = jnp.where(row < n_meta, s, MASK_VALUE)
        m = jnp.max(s, axis=0, keepdims=True)
        p = jnp.exp2(s - m)
        acc_ref[...] = jnp.dot(vTm_ref[0], p.astype(BF16), preferred_element_type=F32)
        return m, jnp.sum(p, axis=0, keepdims=True)

    def chunk(c, carry, *, two_pass):
        m, l, jump = carry
        off = pl.multiple_of(c * tk, tk)
        s = scores(k_ref[0, 0, pl.ds(off, tk), :])
        m_new = jnp.maximum(m, jnp.max(s, axis=0, keepdims=True))
        alpha = jnp.exp2(m - m_new)
        p = jnp.exp2(s - (m_new if two_pass else m))
        ls = jnp.sum(p, axis=0, keepdims=True)
        pv = weighted_values(c * kb_per_chunk, p.astype(BF16))
        if two_pass:
            acc_ref[...] = acc_ref[...] * alpha + pv
            l = l * alpha + ls
        else:
            acc_ref[...] = (acc_ref[...] + pv) * alpha
            l = (l + ls) * alpha
        return m_new, l, jnp.maximum(jump, m_new - m)

    def sweep(*, two_pass):
        m, l = meta_block()
        carry = (m, l, jnp.zeros_like(m))
        if two_pass:
            return lax.fori_loop(0, n_kv, functools.partial(chunk, two_pass=True), carry)

        def pair(jj, carry):
            carry = chunk(2 * jj, carry, two_pass=False)
            return chunk(2 * jj + 1, carry, two_pass=False)

        n_trips = n_kv // 2 + jnp.minimum(pl.program_id(2), 0)
        return lax.fori_loop(0, n_trips, pair, carry)

    _, l, jump = sweep(two_pass=False)
    l_ref[...] = l

    @pl.when(jnp.max(jump) > MAX_JUMP)
    def _():
        _, l, _ = sweep(two_pass=True)
        l_ref[...] = l

    o = (acc_ref[...] / l_ref[...]).T
    o_ref[0] = (o * sz_ref[0].astype(F32)).astype(o_ref.dtype)


def _flash(qT, k, vT, k_meta, vT_meta, sz, *, n_meta, tq_pref=2048, tk_pref=2048):
    bsz, n_heads, n_blocks, _, qb = qT.shape
    s = n_blocks * qb
    tq = _tile(s, tq_pref)
    tk = _tile(s // 2, tk_pref)
    return pl.pallas_call(
        functools.partial(_flash_kernel, tk=tk, n_meta=n_meta),
        out_shape=jax.ShapeDtypeStruct((bsz, s, n_heads * V_DIM), BF16),
        grid=(bsz, n_heads, s // tq),
        in_specs=[pl.BlockSpec((1, 1, tq // qb, QK_PAD, qb), lambda b, h, i: (b, h, i, 0, 0)),
                  pl.BlockSpec((1, 1, s, QK_PAD), lambda b, h, i: (b, h, 0, 0)),
                  pl.BlockSpec((1, 1, n_blocks, V_DIM, qb), lambda b, h, i: (b, h, 0, 0, 0)),
                  pl.BlockSpec((1, META_PAD, QK_PAD), lambda b, h, i: (h, 0, 0)),
                  pl.BlockSpec((1, V_DIM, META_PAD), lambda b, h, i: (h, 0, 0)),
                  pl.BlockSpec((1, tq, V_DIM), lambda b, h, i: (b, i, h))],
        out_specs=pl.BlockSpec((1, tq, V_DIM), lambda b, h, i: (b, i, h)),
        scratch_shapes=[pltpu.VMEM((V_DIM, tq), F32), pltpu.VMEM((1, tq), F32)],
        compiler_params=_params("parallel", "parallel", "arbitrary"),
        name="flash",
    )(qT, k, vT, k_meta, vT_meta, sz)


def _out_kernel(x_ref, oz_ref, u_ref, up_ref, un_ref, um_ref, bz_ref, ga_ref, gc_ref,
                cw_ref, wa_ref, wc_ref, wo_ref, fw_ref, y_ref, *, tiles_per_seq, n_meta):
    i = pl.program_id(0)
    tm = u_ref.shape[0]
    pos = i % tiles_per_seq
    u = u_ref[...].astype(F32)
    halo = up_ref.shape[0]
    prev_row = jnp.where(pos == 0,
                         um_ref[...].astype(F32)[n_meta - 1:n_meta],
                         up_ref[...].astype(F32)[halo - 1:halo])
    next_row = jnp.where(pos == tiles_per_seq - 1,
                         jnp.zeros((1, u.shape[1]), F32),
                         un_ref[...].astype(F32)[0:1])
    rid = lax.broadcasted_iota(jnp.int32, u.shape, 0)
    u_prev = jnp.where(rid == 0, prev_row, pltpu.roll(u, 1, 0))
    u_next = jnp.where(rid == tm - 1, next_row, pltpu.roll(u, tm - 1, 0))
    cw = cw_ref[...]
    conv = cw[0:1] * u_prev + cw[1:2] * u + cw[2:3] * u_next
    yc_in = (bz_ref[...].astype(F32) * conv).astype(BF16)
    y_conv = jnp.dot(yc_in, wc_ref[...], preferred_element_type=F32)
    y_attn = jnp.dot(oz_ref[...], wa_ref[...], preferred_element_type=F32)
    merged = ga_ref[...].astype(F32) * y_attn + gc_ref[...].astype(F32) * y_conv
    out = x_ref[...] + jnp.dot(merged.astype(BF16), wo_ref[...], preferred_element_type=F32)
    y_ref[...] = _rms(out, fw_ref[...])


def _out(x, oz, u, u_meta, bz, g, w, *, seq_len, n_meta, tm_pref=256):
    t, d = x.shape
    cw = u.shape[1]
    aw = oz.shape[1]
    tm = _tile(seq_len, tm_pref)
    halo = 16
    hb = tm // halo
    n_halo = t // halo
    row = lambda i: (i, 0)
    kern = functools.partial(_out_kernel, tiles_per_seq=seq_len // tm, n_meta=n_meta)
    return pl.pallas_call(
        kern,
        out_shape=jax.ShapeDtypeStruct((t, d), F32),
        grid=(t // tm,),
        in_specs=[pl.BlockSpec((tm, d), row),
                  pl.BlockSpec((tm, aw), row),
                  pl.BlockSpec((tm, cw), row),
                  pl.BlockSpec((halo, cw), lambda i: (jnp.maximum(i * hb - 1, 0), 0)),
                  pl.BlockSpec((halo, cw), lambda i: (jnp.minimum((i + 1) * hb, n_halo - 1), 0)),
                  pl.BlockSpec((halo, cw), lambda i: (0, 0)),
                  pl.BlockSpec((tm, cw), row),
                  pl.BlockSpec((tm, d), lambda i: (i, 0)),
                  pl.BlockSpec((tm, d), lambda i: (i, 1)),
                  _resident(w["conv_w"].shape),
                  _resident(w["w_o_attn"].shape),
                  _resident(w["w_o_conv"].shape),
                  _resident(w["w_o"].shape),
                  _resident((1, d))],
        out_specs=pl.BlockSpec((tm, d), row),
        compiler_params=_params("parallel"),
        name="out",
    )(x, oz, u, u, u, u_meta, bz, g, g, w["conv_w"], w["w_o_attn"], w["w_o_conv"], w["w_o"],
      w["final_norm"])


def _prep_weights(norm_w, w_in, b_gate, q_a_norm_w, w_uq, kv_a_norm_w, w_ukv,
                  w_o_attn, conv_w, w_o_conv, w_o, final_norm_w):
    d = w_in.shape[1]
    q_rank = q_a_norm_w.shape[-1]
    kv_rank = kv_a_norm_w.shape[-1]
    n_heads = w_uq.shape[-1] // QK_DIM
    aw = w_o_attn.shape[1]
    cwid = w_o_conv.shape[1]
    sizes = (q_rank, kv_rank, QK_ROPE_DIM, aw, cwid, cwid, cwid, cwid, 2 * d)
    assert sum(sizes) == w_in.shape[-1]
    cols, start = [], 0
    for sz in sizes:
        cols.append((start, start + sz))
        start += sz
    wt = jnp.transpose(w_in[0])
    cut = lambda k: wt[cols[k][0]:cols[k][1]].astype(BF16)
    wuq = w_uq[0].reshape(q_rank, n_heads, QK_DIM).transpose(1, 2, 0)
    wukv = w_ukv[0].reshape(kv_rank, n_heads, QK_NOPE_DIM + V_DIM)
    return dict(
        n_heads=n_heads, q_rank=q_rank, kv_rank=kv_rank,
        norm=norm_w[0],
        w_small=jnp.transpose(wt[:cols[2][1]].astype(BF16)),
        w_z=(wt[cols[3][0]:cols[3][1]] * norm_w[0][None, :]).astype(BF16),
        w_cx=cut(4), w_cb=cut(5), w_cc=cut(6), w_zc=cut(7), w_g=cut(8),
        b_gate=b_gate[0],
        q_norm=q_a_norm_w[0].reshape(1, q_rank),
        kv_norm=kv_a_norm_w[0].reshape(1, kv_rank),
        w_uqT=wuq.reshape(n_heads * QK_DIM, q_rank).astype(BF16),
        w_uk=wukv[:, :, :QK_NOPE_DIM].reshape(kv_rank, n_heads * QK_NOPE_DIM).astype(BF16),
        w_uvT=wukv[:, :, QK_NOPE_DIM:].transpose(1, 2, 0).reshape(n_heads * V_DIM, kv_rank).astype(BF16),
        w_o_attn=w_o_attn[0].astype(BF16),
        conv_w=conv_w[0],
        w_o_conv=w_o_conv[0].astype(BF16),
        w_o=w_o[0].astype(BF16),
        final_norm=final_norm_w.reshape(1, d),
    )


def _meta_state(meta_tokens, w):
    n_meta, d = meta_tokens.shape
    xm = jnp.pad(meta_tokens, ((0, META_PAD - n_meta), (0, 0)))
    xn, _ = _norm_silu(xm, w["w_z"], w["norm"])
    _, k_meta, vT_meta = _attn_prep(xn[None], w, _rope_tables(0, META_PAD))
    u_meta = _proj(xn, w["w_cx"], w["w_cc"], "mul")
    return k_meta[0], vT_meta[0, :, 0], u_meta


def _trunk(x, meta, w, n_meta):
    bsz, s, d = x.shape
    k_meta, vT_meta, u_meta = meta
    x2 = x.reshape(bsz * s, d)
    xn, sz = _norm_silu(x2, w["w_z"], w["norm"])
    qT, k, vT = _attn_prep(xn.reshape(bsz, s, d), w, _rope_tables(n_meta, s))
    u =_proj(xn, w["w_cx"], w["w_cc"], "mul")
    bz = _proj(xn, w["w_cb"], w["w_zc"], "mul_silu")
    g = _proj(xn, w["w_g"], w["b_gate"], "sigmoid_bias")
    oz = _flash(qT, k, vT, k_meta, vT_meta, sz.reshape(bsz, s, -1), n_meta=n_meta)
    y = _out(x2, oz.reshape(bsz * s, -1), u, u_meta, bz, g, w, seq_len=s, n_meta=n_meta)
    return y.reshape(bsz, s, d)


def kernel(x_prompt, x_sample, meta_tokens, norm_w, w_in, b_gate, q_a_norm_w, w_uq, kv_a_norm_w, w_ukv, w_o_attn, conv_w, w_o_conv, w_o, final_norm_w):
    assert norm_w.shape[0] == 1
    n_meta = meta_tokens.shape[0]
    assert n_meta <= 16
    w = _prep_weights(norm_w, w_in, b_gate, q_a_norm_w, w_uq, kv_a_norm_w, w_ukv,
                      w_o_attn, conv_w, w_o_conv, w_o, final_norm_w)
    meta = _meta_state(meta_tokens, w)
    return (_trunk(x_prompt, meta, w, n_meta), _trunk(x_sample, meta, w, n_meta))
```

```python
import functools

import jax
import jax.numpy as jnp
from jax import lax
from jax.experimental import pallas as pl
from jax.experimental.pallas import tpu as pltpu

QK_NOPE_DIM = 128
QK_ROPE_DIM = 64
QK_DIM = QK_NOPE_DIM + QK_ROPE_DIM
V_DIM = 128
ROPE_HALF = QK_ROPE_DIM // 2
QK_PAD = 256
ROPE_THETA = 10000.0
NORM_EPS = 1e-6
MASK_VALUE = -1e30
MAX_JUMP = 64.0
LOG2_E = 1.4426950408889634
META_PAD = 128
VMEM_LIMIT_BYTES = 56 * 1024 * 1024

F32 = jnp.float32
BF16 = jnp.bfloat16


def _params(*semantics):
    return pltpu.CompilerParams(dimension_semantics=semantics,
                                vmem_limit_bytes=VMEM_LIMIT_BYTES)


def _tile(n, pref):
    if n <= pref:
        return n
    t = pref
    while n % t:
        t //= 2
    return t


def _nt_dot(a, b):
    return lax.dot_general(a, b, (((1,), (1,)), ((), ())), preferred_element_type=F32)


def _rms(x, w):
    return x * lax.rsqrt(jnp.mean(x * x, axis=-1, keepdims=True) + NORM_EPS) * w


def _resident(shape):
    nd = len(shape)
    return pl.BlockSpec(shape, lambda *_: (0,) * nd, pipeline_mode=pl.Buffered(1))


def _norm_silu_kernel(x_ref, wz_ref, nw_ref, xn_ref, sz_ref):
    x = x_ref[...]
    r = lax.rsqrt(jnp.mean(x * x, axis=-1, keepdims=True) + NORM_EPS)
    z = _nt_dot(x.astype(BF16), wz_ref[...]) * r
    sz_ref[...] = jax.nn.silu(z).astype(sz_ref.dtype)

    @pl.when(pl.program_id(1) == 0)
    def _():
        xn_ref[...] = (x * r * nw_ref[...]).astype(xn_ref.dtype)


def _norm_silu(x, wz_folded, norm_w, *, tm_pref=1024, tn_pref=1024):
    t, d = x.shape
    n = wz_folded.shape[0]
    tm, tn = _tile(t, tm_pref), _tile(n, tn_pref)
    return pl.pallas_call(
        _norm_silu_kernel,
        out_shape=(jax.ShapeDtypeStruct((t, d), BF16), jax.ShapeDtypeStruct((t, n), BF16)),
        grid=(t // tm, n // tn),
        in_specs=[pl.BlockSpec((tm, d), lambda i, j: (i, 0)),
                  pl.BlockSpec((tn, d), lambda i, j: (j, 0)),
                  pl.BlockSpec((1, d), lambda i, j: (0, 0))],
        out_specs=(pl.BlockSpec((tm, d), lambda i, j: (i, 0)),
                   pl.BlockSpec((tm, tn), lambda i, j: (i, j))),
        compiler_params=_params("parallel", "arbitrary"),
        name="norm_silu",
    )(x, wz_folded, norm_w.reshape(1, d))


def _attn_prep_kernel(xn_ref, wsm_ref, qnw_ref, kvnw_ref, wuqT_ref, wuk_ref, wuvT_ref,
                      cosT_ref, sinT_ref,
                      qT_ref, k_ref, vT_ref, *, n_heads, q_rank, kv_rank, scale, sub):
    tm = xn_ref.shape[1]
    q_zero = jnp.zeros((QK_PAD - QK_DIM, sub), BF16)
    k_zero = jnp.zeros((sub, QK_PAD - QK_DIM), BF16)
    r1 = QK_NOPE_DIM + ROPE_HALF

    def group(g, carry):
        t0 = pl.multiple_of(g * sub, sub)
        tok = pl.ds(t0, sub)
        p = jnp.dot(xn_ref[0, tok, :], wsm_ref[...], preferred_element_type=F32)
        qn = _rms(p[:, :q_rank], qnw_ref[...]).astype(BF16)
        cn = _rms(p[:, q_rank:q_rank + kv_rank], kvnw_ref[...]).astype(BF16)
        kr = p[:, q_rank + kv_rank:]
        kr_sw = jnp.concatenate([kr[:, ROPE_HALF:], kr[:, :ROPE_HALF]], axis=1)
        c = cosT_ref[:, tok]
        s = sinT_ref[:, tok]
        ck, sk = c.T, s.T
        kr_rot = (kr * jnp.concatenate([ck, ck], axis=1)
                  + kr_sw * jnp.concatenate([-sk, sk], axis=1)).astype(BF16)

        qT = _nt_dot(wuqT_ref[...], qn) * scale
        k_nope = jnp.dot(cn, wuk_ref[...], preferred_element_type=F32)
        vT = _nt_dot(wuvT_ref[...], cn)
        for h in range(n_heads):
            b = h * QK_DIM
            x1 = qT[b + QK_NOPE_DIM:b + r1]
            x2 = qT[b + r1:b + QK_DIM]
            qT_ref[0, h, 0, 0:QK_NOPE_DIM, tok] = qT[b:b + QK_NOPE_DIM].astype(BF16)
            qT_ref[0, h, 0, QK_NOPE_DIM:r1, tok] = (x1 * c - x2 * s).astype(BF16)
            qT_ref[0, h, 0, r1:QK_DIM, tok] = (x2 * c + x1 * s).astype(BF16)
            qT_ref[0, h, 0, QK_DIM:QK_PAD, tok] = q_zero
            k_ref[0, h, tok, 0:QK_NOPE_DIM] = k_nope[:, h * QK_NOPE_DIM:(h + 1) * QK_NOPE_DIM].astype(BF16)
            k_ref[0, h, tok, QK_NOPE_DIM:QK_DIM] = kr_rot
            k_ref[0, h, tok, QK_DIM:QK_PAD] = k_zero
            vT_ref[0, h, 0, :, tok] = vT[h * V_DIM:(h + 1) * V_DIM].astype(BF16)
        return carry

    lax.fori_loop(0, tm // sub, group, 0)


def _attn_prep(xn, w, rope, *, tm_pref=512, sub_pref=256):
    bsz, s, d = xn.shape
    n_heads = w["n_heads"]
    q_rank, kv_rank = w["q_rank"], w["kv_rank"]
    tm = _tile(s, tm_pref)
    sub = _tile(tm, sub_pref)
    cosT, sinT = rope
    kern = functools.partial(_attn_prep_kernel, n_heads=n_heads, q_rank=q_rank,
                             kv_rank=kv_rank, scale=QK_DIM ** -0.5 * LOG2_E, sub=sub)
    return pl.pallas_call(
        kern,
        out_shape=(jax.ShapeDtypeStruct((bsz, n_heads, s // tm, QK_PAD, tm), BF16),
                   jax.ShapeDtypeStruct((bsz, n_heads, s, QK_PAD), BF16),
                   jax.ShapeDtypeStruct((bsz, n_heads, s // tm, V_DIM, tm), BF16)),
        grid=(bsz, s // tm),
        in_specs=[pl.BlockSpec((1, tm, d), lambda b, i: (b, i, 0)),
                  _resident(w["w_small"].shape),
                  _resident((1, q_rank)),
                  _resident((1, kv_rank)),
                  _resident(w["w_uqT"].shape),
                  _resident(w["w_uk"].shape),
                  _resident(w["w_uvT"].shape),
                  pl.BlockSpec((ROPE_HALF, tm), lambda b, i: (0, i)),
                  pl.BlockSpec((ROPE_HALF, tm), lambda b, i: (0, i))],
        out_specs=(pl.BlockSpec((1, n_heads, 1, QK_PAD, tm), lambda b, i: (b, 0, i, 0, 0)),
                   pl.BlockSpec((1, n_heads, tm, QK_PAD), lambda b, i: (b, 0, i, 0)),
                   pl.BlockSpec((1, n_heads, 1, V_DIM, tm), lambda b, i: (b, 0, i, 0, 0))),
        compiler_params=_params("parallel", "parallel"),
        name="attn_prep",
    )(xn, w["w_small"], w["q_norm"], w["kv_norm"], w["w_uqT"], w["w_uk"], w["w_uvT"],
      cosT, sinT)


def _rope_tables(start, length):
    inv_freq = 1.0 / (ROPE_THETA ** (jnp.arange(0, QK_ROPE_DIM, 2, dtype=F32) / QK_ROPE_DIM))
    pos = jnp.arange(start, start + length, dtype=F32)
    ang = pos[:, None] * inv_freq[None, :]
    return jnp.cos(ang).T, jnp.sin(ang).T


def _proj_kernel(*refs, mode):
    xn_ref, w1_ref = refs[0], refs[1]
    o_ref = refs[-1]
    xn = xn_ref[...]
    p1 = _nt_dot(xn, w1_ref[...])
    if mode == "sigmoid_bias":
        out = jax.nn.sigmoid(p1 + refs[2][...])
    else:
        p2 = _nt_dot(xn, refs[2][...])
        out = p1 * p2 if mode == "mul" else p1 * jax.nn.silu(p2)
    o_ref[...] = out.astype(o_ref.dtype)


def _proj(xn, w1, second, mode, *, tm_pref=1024, tn_pref=1024):
    t, d = xn.shape
    n = w1.shape[0]
    single = mode == "sigmoid_bias"
    tm, tn = _tile(t, tm_pref), _tile(n, 2 * tn_pref if single else tn_pref)
    w_spec = pl.BlockSpec((tn, d), lambda i, j: (j, 0))
    in_specs = [pl.BlockSpec((tm, d), lambda i, j: (i, 0)), w_spec]
    args = [xn, w1]
    if single:
        in_specs.append(pl.BlockSpec((1, tn), lambda i, j: (0, j)))
        args.append(second.reshape(1, n))
    else:
        in_specs.append(w_spec)
        args.append(second)
    return pl.pallas_call(
        functools.partial(_proj_kernel, mode=mode),
        out_shape=jax.ShapeDtypeStruct((t, n), BF16),
        grid=(t // tm, n // tn),
        in_specs=in_specs,
        out_specs=pl.BlockSpec((tm, tn), lambda i, j: (i, j)),
        compiler_params=_params("parallel", "parallel"),
        name="proj_" + mode,
    )(*args)


def _flash_kernel(qT_ref, k_ref, vT_ref, km_ref, vTm_ref, sz_ref, o_ref, acc_ref, l_ref,
                  *, tk, n_meta):
    n_qb, _, qb = qT_ref.shape[2:]
    n_kv = k_ref.shape[2] // tk
    kb_per_chunk = tk // vT_ref.shape[-1]

    def scores(k):
        return jnp.concatenate(
            [jnp.dot(k, qT_ref[0, 0, g], preferred_element_type=F32) for g in range(n_qb)], axis=1)

    def weighted_values(first_block, p):
        vb = vT_ref.shape[-1]
        return sum(jnp.dot(vT_ref[0, 0, first_block + j], p[j * vb:(j + 1) * vb],
                           preferred_element_type=F32) for j in range(p.shape[0] // vb))

    def meta_block():
        s = scores(km_ref[0])
        row = lax.broadcasted_iota(jnp.int32, s.shape, 0)
        s = jnp.where(row < n_meta, s, MASK_VALUE)
        m = jnp.max(s, axis=0, keepdims=True)
        p = jnp.exp2(s - m)
        acc_ref[...] = jnp.dot(vTm_ref[0], p.astype(BF16), preferred_element_type=F32)
        return m, jnp.sum(p, axis=0, keepdims=True)

    def chunk(c, carry, *, two_pass):
        m, l, jump = carry
        off = pl.multiple_of(c * tk, tk)
        s = scores(k_ref[0, 0, pl.ds(off, tk), :])
        m_new = jnp.maximum(m, jnp.max(s, axis=0, keepdims=True))
        alpha = jnp.exp2(m - m_new)
        p = jnp.exp2(s - (m_new if two_pass else m))
        ls = jnp.sum(p, axis=0, keepdims=True)
        pv = weighted_values(c * kb_per_chunk, p.astype(BF16))
        if two_pass:
            acc_ref[...] = acc_ref[...] * alpha + pv
            l = l * alpha + ls
        else:
            acc_ref[...] = (acc_ref[...] + pv) * alpha
            l = (l + ls) * alpha
        return m_new, l, jnp.maximum(jump, m_new - m)

    def sweep(*, two_pass):
        m, l = meta_block()
        carry = (m, l, jnp.zeros_like(m))
        if two_pass:
            return lax.fori_loop(0, n_kv, functools.partial(chunk, two_pass=True), carry)

        def pair(jj, carry):
            carry = chunk(2 * jj, carry, two_pass=False)
            return chunk(2 * jj + 1, carry, two_pass=False)

        n_trips = n_kv // 2 + jnp.minimum(pl.program_id(2), 0)
        return lax.fori_loop(0, n_trips, pair, carry)

    _, l, jump = sweep(two_pass=False)
    l_ref[...] = l

    @pl.when(jnp.max(jump) > MAX_JUMP)
    def _():
        _, l, _ = sweep(two_pass=True)
        l_ref[...] = l

    o = (acc_ref[...] / l_ref[...]).T
    o_ref[0] = (o * sz_ref[0].astype(F32)).astype(o_ref.dtype)


def _flash(qT, k, vT, k_meta, vT_meta, sz, *, n_meta, tq_pref=2048, tk_pref=2048):
    bsz, n_heads, n_blocks, _, qb = qT.shape
    s = n_blocks * qb
    tq = _tile(s, tq_pref)
    tk = _tile(s // 2, tk_pref)
    return pl.pallas_call(
        functools.partial(_flash_kernel, tk=tk, n_meta=n_meta),
        out_shape=jax.ShapeDtypeStruct((bsz, s, n_heads * V_DIM), BF16),
        grid=(bsz, n_heads, s // tq),
        in_specs=[pl.BlockSpec((1, 1, tq // qb, QK_PAD, qb), lambda b, h, i: (b, h, i, 0, 0)),
                  pl.BlockSpec((1, 1, s, QK_PAD), lambda b, h, i: (b, h, 0, 0)),
                  pl.BlockSpec((1, 1, n_blocks, V_DIM, qb), lambda b, h, i: (b, h, 0, 0, 0)),
                  pl.BlockSpec((1, META_PAD, QK_PAD), lambda b, h, i: (h, 0, 0)),
                  pl.BlockSpec((1, V_DIM, META_PAD), lambda b, h, i: (h, 0, 0)),
                  pl.BlockSpec((1, tq, V_DIM), lambda b, h, i: (b, i, h))],
        out_specs=pl.BlockSpec((1, tq, V_DIM), lambda b, h, i: (b, i, h)),
        scratch_shapes=[pltpu.VMEM((V_DIM, tq), F32), pltpu.VMEM((1, tq), F32)],
        compiler_params=_params("parallel", "parallel", "arbitrary"),
        name="flash",
    )(qT, k, vT, k_meta, vT_meta, sz)


def _out_kernel(x_ref, oz_ref, u_ref, up_ref, un_ref, um_ref, bz_ref, ga_ref, gc_ref,
                cw_ref, wa_ref, wc_ref, wo_ref, fw_ref, y_ref, *, tiles_per_seq, n_meta):
    i = pl.program_id(0)
    tm = u_ref.shape[0]
    pos = i % tiles_per_seq
    u = u_ref[...].astype(F32)
    halo = up_ref.shape[0]
    prev_row = jnp.where(pos == 0,
                         um_ref[...].astype(F32)[n_meta - 1:n_meta],
                         up_ref[...].astype(F32)[halo - 1:halo])
    next_row = jnp.where(pos == tiles_per_seq - 1,
                         jnp.zeros((1, u.shape[1]), F32),
                         un_ref[...].astype(F32)[0:1])
    rid = lax.broadcasted_iota(jnp.int32, u.shape, 0)
    u_prev = jnp.where(rid == 0, prev_row, pltpu.roll(u, 1, 0))
    u_next = jnp.where(rid == tm - 1, next_row, pltpu.roll(u, tm - 1, 0))
    cw = cw_ref[...]
    conv = cw[0:1] * u_prev + cw[1:2] * u + cw[2:3] * u_next
    yc_in = (bz_ref[...].astype(F32) * conv).astype(BF16)
    y_conv = jnp.dot(yc_in, wc_ref[...], preferred_element_type=F32)
    y_attn = jnp.dot(oz_ref[...], wa_ref[...], preferred_element_type=F32)
    merged = ga_ref[...].astype(F32) * y_attn + gc_ref[...].astype(F32) * y_conv
    out = x_ref[...] + jnp.dot(merged.astype(BF16), wo_ref[...], preferred_element_type=F32)
    y_ref[...] = _rms(out, fw_ref[...])


def _out(x, oz, u, u_meta, bz, g, w, *, seq_len, n_meta, tm_pref=256):
    t, d = x.shape
    cw = u.shape[1]
    aw = oz.shape[1]
    tm = _tile(seq_len, tm_pref)
    halo = 16
    hb = tm // halo
    n_halo = t // halo
    row = lambda i: (i, 0)
    kern = functools.partial(_out_kernel, tiles_per_seq=seq_len // tm, n_meta=n_meta)
    return pl.pallas_call(
        kern,
        out_shape=jax.ShapeDtypeStruct((t, d), F32),
        grid=(t // tm,),
        in_specs=[pl.BlockSpec((tm, d), row),
                  pl.BlockSpec((tm, aw), row),
                  pl.BlockSpec((tm, cw), row),
                  pl.BlockSpec((halo, cw), lambda i: (jnp.maximum(i * hb - 1, 0), 0)),
                  pl.BlockSpec((halo, cw), lambda i: (jnp.minimum((i + 1) * hb, n_halo - 1), 0)),
                  pl.BlockSpec((halo, cw), lambda i: (0, 0)),
                  pl.BlockSpec((tm, cw), row),
                  pl.BlockSpec((tm, d), lambda i: (i, 0)),
                  pl.BlockSpec((tm, d), lambda i: (i, 1)),
                  _resident(w["conv_w"].shape),
                  _resident(w["w_o_attn"].shape),
                  _resident(w["w_o_conv"].shape),
                  _resident(w["w_o"].shape),
                  _resident((1, d))],
        out_specs=pl.BlockSpec((tm, d), row),
        compiler_params=_params("parallel"),
        name="out",
    )(x, oz, u, u, u, u_meta, bz, g, g, w["conv_w"], w["w_o_attn"], w["w_o_conv"], w["w_o"],
      w["final_norm"])


def _prep_weights(norm_w, w_in, b_gate, q_a_norm_w, w_uq, kv_a_norm_w, w_ukv,
                  w_o_attn, conv_w, w_o_conv, w_o, final_norm_w):
    d = w_in.shape[1]
    q_rank = q_a_norm_w.shape[-1]
    kv_rank = kv_a_norm_w.shape[-1]
    n_heads = w_uq.shape[-1] // QK_DIM
    aw = w_o_attn.shape[1]
    cwid = w_o_conv.shape[1]
    sizes = (q_rank, kv_rank, QK_ROPE_DIM, aw, cwid, cwid, cwid, cwid, 2 * d)
    assert sum(sizes) == w_in.shape[-1]
    cols, start = [], 0
    for sz in sizes:
        cols.append((start, start + sz))
        start += sz
    wt = jnp.transpose(w_in[0])
    cut = lambda k: wt[cols[k][0]:cols[k][1]].astype(BF16)
    wuq = w_uq[0].reshape(q_rank, n_heads, QK_DIM).transpose(1, 2, 0)
    wukv = w_ukv[0].reshape(kv_rank, n_heads, QK_NOPE_DIM + V_DIM)
    return dict(
        n_heads=n_heads, q_rank=q_rank, kv_rank=kv_rank,
        norm=norm_w[0],
        w_small=jnp.transpose(wt[:cols[2][1]].astype(BF16)),
        w_z=(wt[cols[3][0]:cols[3][1]] * norm_w[0][None, :]).astype(BF16),
        w_cx=cut(4), w_cb=cut(5), w_cc=cut(6), w_zc=cut(7), w_g=cut(8),
        b_gate=b_gate[0],
        q_norm=q_a_norm_w[0].reshape(1, q_rank),
        kv_norm=kv_a_norm_w[0].reshape(1, kv_rank),
        w_uqT=wuq.reshape(n_heads * QK_DIM, q_rank).astype(BF16),
        w_uk=wukv[:, :, :QK_NOPE_DIM].reshape(kv_rank, n_heads * QK_NOPE_DIM).astype(BF16),
        w_uvT=wukv[:, :, QK_NOPE_DIM:].transpose(1, 2, 0).reshape(n_heads * V_DIM, kv_rank).astype(BF16),
        w_o_attn=w_o_attn[0].astype(BF16),
        conv_w=conv_w[0],
        w_o_conv=w_o_conv[0].astype(BF16),
        w_o=w_o[0].astype(BF16),
        final_norm=final_norm_w.reshape(1, d),
    )


def _meta_state(meta_tokens, w):
    n_meta, d = meta_tokens.shape
    xm = jnp.pad(meta_tokens, ((0, META_PAD - n_meta), (0, 0)))
    xn, _ = _norm_silu(xm, w["w_z"], w["norm"])
    _, k_meta, vT_meta = _attn_prep(xn[None], w, _rope_tables(0, META_PAD))
    u_meta = _proj(xn, w["w_cx"], w["w_cc"], "mul")
    return k_meta[0], vT_meta[0, :, 0], u_meta


def _trunk(x, meta, w, n_meta):
    bsz, s, d = x.shape
    k_meta, vT_meta, u_meta = meta
    x2 = x.reshape(bsz * s, d)
    xn, sz = _norm_silu(x2, w["w_z"], w["norm"])
    qT, k, vT = _attn_prep(xn.reshape(bsz, s, d), w, _rope_tables(n_meta, s))
    u =_proj(xn, w["w_cx"], w["w_cc"], "mul")
    bz = _proj(xn, w["w_cb"], w["w_zc"], "mul_silu")
    g = _proj(xn, w["w_g"], w["b_gate"], "sigmoid_bias")
    oz = _flash(qT, k, vT, k_meta, vT_meta, sz.reshape(bsz, s, -1), n_meta=n_meta)
    y = _out(x2, oz.reshape(bsz * s, -1), u, u_meta, bz, g, w, seq_len=s, n_meta=n_meta)
    return y.reshape(bsz, s, d)


def kernel(x_prompt, x_sample, meta_tokens, norm_w, w_in, b_gate, q_a_norm_w, w_uq, kv_a_norm_w, w_ukv, w_o_attn, conv_w, w_o_conv, w_o, final_norm_w):
    assert norm_w.shape[0] == 1
    n_meta = meta_tokens.shape[0]
    assert n_meta <= 16
    w = _prep_weights(norm_w, w_in, b_gate, q_a_norm_w, w_uq, kv_a_norm_w, w_ukv,
                      w_o_attn, conv_w, w_o_conv, w_o, final_norm_w)
    meta = _meta_state(meta_tokens, w)
    return (_trunk(x_prompt, meta, w, n_meta), _trunk(x_sample, meta, w, n_meta))
```

```python
import functools

import jax
import jax.numpy as jnp
from jax import lax
from jax.experimental import pallas as pl
from jax.experimental.pallas import tpu as pltpu

QK_NOPE_DIM = 128
QK_ROPE_DIM = 64
QK_DIM = QK_NOPE_DIM + QK_ROPE_DIM
V_DIM = 128
ROPE_HALF = QK_ROPE_DIM // 2
QK_PAD = 256
ROPE_THETA = 10000.0
NORM_EPS = 1e-6
MASK_VALUE = -1e30
MAX_JUMP = 64.0
LOG2_E = 1.4426950408889634
META_PAD = 128
VMEM_LIMIT_BYTES = 56 * 1024 * 1024

F32 = jnp.float32
BF16 = jnp.bfloat16


def _params(*semantics):
    return pltpu.CompilerParams(dimension_semantics=semantics,
                                vmem_limit_bytes=VMEM_LIMIT_BYTES)


def _tile(n, pref):
    if n <= pref:
        return n
    t = pref
    while n % t:
        t //= 2
    return t


def _nt_dot(a, b):
    return lax.dot_general(a, b, (((1,), (1,)), ((), ())), preferred_element_type=F32)


def _rms(x, w):
    return x * lax.rsqrt(jnp.mean(x * x, axis=-1, keepdims=True) + NORM_EPS) * w


def _resident(shape):
    nd = len(shape)
    return pl.BlockSpec(shape, lambda *_: (0,) * nd, pipeline_mode=pl.Buffered(1))


def _norm_silu_kernel(x_ref, wz_ref, nw_ref, xn_ref, sz_ref):
    x = x_ref[...]
    r = lax.rsqrt(jnp.mean(x * x, axis=-1, keepdims=True) + NORM_EPS)
    z = _nt_dot(x.astype(BF16), wz_ref[...]) * r
    sz_ref[...] = jax.nn.silu(z).astype(sz_ref.dtype)

    @pl.when(pl.program_id(1) == 0)
    def _():
        xn_ref[...] = (x * r * nw_ref[...]).astype(xn_ref.dtype)


def _w_rows_spec(view, tn, d):
    _, first_row, n_rows = view
    assert first_row % tn == 0 and n_rows % tn == 0
    return pl.BlockSpec((tn, d), lambda i, j: (first_row // tn + j, 0))


def _norm_silu(x, wz_folded, norm_w, *, tm_pref=1024, tn_pref=1024):
    t, d = x.shape
    n = wz_folded[2]
    tm, tn = _tile(t, tm_pref), _tile(n, tn_pref)
    return pl.pallas_call(
        _norm_silu_kernel,
        out_shape=(jax.ShapeDtypeStruct((t, d), BF16), jax.ShapeDtypeStruct((t, n), BF16)),
        grid=(t // tm, n // tn),
        in_specs=[pl.BlockSpec((tm, d), lambda i, j: (i, 0)),
                  _w_rows_spec(wz_folded, tn, d),
                  pl.BlockSpec((1, d), lambda i, j: (0, 0))],
        out_specs=(pl.BlockSpec((tm, d), lambda i, j: (i, 0)),
                   pl.BlockSpec((tm, tn), lambda i, j: (i, j))),
        compiler_params=_params("parallel", "arbitrary"),
        name="norm_silu",
    )(x, wz_folded[0], norm_w.reshape(1, d))


def _attn_prep_kernel(xn_ref, wsm_ref, qnw_ref, kvnw_ref, wuqT_ref, wuk_ref, wuvT_ref,
                      cosT_ref, sinT_ref,
                      qT_ref, k_ref, vT_ref, *, n_heads, q_rank, kv_rank, scale, sub):
    tm = xn_ref.shape[1]
    q_zero = jnp.zeros((QK_PAD - QK_DIM, sub), BF16)
    k_zero = jnp.zeros((sub, QK_PAD - QK_DIM), BF16)
    r1 = QK_NOPE_DIM + ROPE_HALF

    def group(g, carry):
        t0 = pl.multiple_of(g * sub, sub)
        tok = pl.ds(t0, sub)
        p = jnp.dot(xn_ref[0, tok, :], wsm_ref[...], preferred_element_type=F32)
        qn = _rms(p[:, :q_rank], qnw_ref[...]).astype(BF16)
        cn = _rms(p[:, q_rank:q_rank + kv_rank], kvnw_ref[...]).astype(BF16)
        kr = p[:, q_rank + kv_rank:]
        kr_sw = jnp.concatenate([kr[:, ROPE_HALF:], kr[:, :ROPE_HALF]], axis=1)
        c = cosT_ref[:, tok]
        s = sinT_ref[:, tok]
        ck, sk = c.T, s.T
        kr_rot = (kr * jnp.concatenate([ck, ck], axis=1)
                  + kr_sw * jnp.concatenate([-sk, sk], axis=1)).astype(BF16)

        qT = _nt_dot(wuqT_ref[...], qn) * scale
        k_nope = jnp.dot(cn, wuk_ref[...], preferred_element_type=F32)
        vT = _nt_dot(wuvT_ref[...], cn)
        for h in range(n_heads):
            b = h * QK_DIM
            x1 = qT[b + QK_NOPE_DIM:b + r1]
            x2 = qT[b + r1:b + QK_DIM]
            qT_ref[0, h, 0, 0:QK_NOPE_DIM, tok] = qT[b:b + QK_NOPE_DIM].astype(BF16)
            qT_ref[0, h, 0, QK_NOPE_DIM:r1, tok] = (x1 * c - x2 * s).astype(BF16)
            qT_ref[0, h, 0, r1:QK_DIM, tok] = (x2 * c + x1 * s).astype(BF16)
            qT_ref[0, h, 0, QK_DIM:QK_PAD, tok] = q_zero
            k_ref[0, h, tok, 0:QK_NOPE_DIM] = k_nope[:, h * QK_NOPE_DIM:(h + 1) * QK_NOPE_DIM].astype(BF16)
            k_ref[0, h, tok, QK_NOPE_DIM:QK_DIM] = kr_rot
            k_ref[0, h, tok, QK_DIM:QK_PAD] = k_zero
            vT_ref[0, h, 0, :, tok] = vT[h * V_DIM:(h + 1) * V_DIM].astype(BF16)
        return carry

    lax.fori_loop(0, tm // sub, group, 0)


def _attn_prep(xn, w, rope, *, tm_pref=512, sub_pref=256):
    bsz, s, d = xn.shape
    n_heads = w["n_heads"]
    q_rank, kv_rank = w["q_rank"], w["kv_rank"]
    tm = _tile(s, tm_pref)
    sub = _tile(tm, sub_pref)
    cosT, sinT = rope
    kern = functools.partial(_attn_prep_kernel, n_heads=n_heads, q_rank=q_rank,
                             kv_rank=kv_rank, scale=QK_DIM ** -0.5 * LOG2_E, sub=sub)
    return pl.pallas_call(
        kern,
        out_shape=(jax.ShapeDtypeStruct((bsz, n_heads, s // tm, QK_PAD, tm), BF16),
                   jax.ShapeDtypeStruct((bsz, n_heads, s, QK_PAD), BF16),
                   jax.ShapeDtypeStruct((bsz, n_heads, s // tm, V_DIM, tm), BF16)),
        grid=(bsz, s // tm),
        in_specs=[pl.BlockSpec((1, tm, d), lambda b, i: (b, i, 0)),
                  _resident(w["w_small"].shape),
                  _resident((1, q_rank)),
                  _resident((1, kv_rank)),
                  _resident(w["w_uqT"].shape),
                  _resident(w["w_uk"].shape),
                  _resident(w["w_uvT"].shape),
                  pl.BlockSpec((ROPE_HALF, tm), lambda b, i: (0, i)),
                  pl.BlockSpec((ROPE_HALF, tm), lambda b, i: (0, i))],
        out_specs=(pl.BlockSpec((1, n_heads, 1, QK_PAD, tm), lambda b, i: (b, 0, i, 0, 0)),
                   pl.BlockSpec((1, n_heads, tm, QK_PAD), lambda b, i: (b, 0, i, 0)),
                   pl.BlockSpec((1, n_heads, 1, V_DIM, tm), lambda b, i: (b, 0, i, 0, 0))),
        compiler_params=_params("parallel", "parallel"),
        name="attn_prep",
    )(xn, w["w_small"], w["q_norm"], w["kv_norm"], w["w_uqT"], w["w_uk"], w["w_uvT"],
      cosT, sinT)


def _rope_tables(start, length):
    inv_freq = 1.0 / (ROPE_THETA ** (jnp.arange(0, QK_ROPE_DIM, 2, dtype=F32) / QK_ROPE_DIM))
    pos = jnp.arange(start, start + length, dtype=F32)
    ang = pos[:, None] * inv_freq[None, :]
    return jnp.cos(ang).T, jnp.sin(ang).T


def _proj_kernel(*refs, mode):
    xn_ref, w1_ref = refs[0], refs[1]
    o_ref = refs[-1]
    xn = xn_ref[...]
    p1 = _nt_dot(xn, w1_ref[...])
    if mode == "sigmoid_bias":
        out = jax.nn.sigmoid(p1 + refs[2][...])
    else:
        p2 = _nt_dot(xn, refs[2][...])
        out = p1 * p2 if mode == "mul" else p1 * jax.nn.silu(p2)
    o_ref[...] = out.astype(o_ref.dtype)


def _proj(xn, w1, second, mode, *, tm_pref=1024, tn_pref=1024):
    t, d = xn.shape
    n = w1[2]
    single = mode == "sigmoid_bias"
    tm, tn = _tile(t, tm_pref), _tile(n, 2 * tn_pref if single else tn_pref)
    in_specs = [pl.BlockSpec((tm, d), lambda i, j: (i, 0)), _w_rows_spec(w1, tn, d)]
    args = [xn, w1[0]]
    if single:
        in_specs.append(pl.BlockSpec((1, tn), lambda i, j: (0, j)))
        args.append(second.reshape(1, n))
    else:
        in_specs.append(_w_rows_spec(second, tn, d))
        args.append(second[0])
    return pl.pallas_call(
        functools.partial(_proj_kernel, mode=mode),
        out_shape=jax.ShapeDtypeStruct((t, n), BF16),
        grid=(t // tm, n // tn),
        in_specs=in_specs,
        out_specs=pl.BlockSpec((tm, tn), lambda i, j: (i, j)),
        compiler_params=_params("parallel", "parallel"),
        name="proj_" + mode,
    )(*args)


def _flash_kernel(qT_ref, k_ref, vT_ref, km_ref, vTm_ref, sz_ref, o_ref, acc_ref, l_ref,
                  *, tk, n_meta):
    n_qb, _, qb = qT_ref.shape[2:]
    n_kv = k_ref.shape[2] // tk
    kb_per_chunk = tk // vT_ref.shape[-1]

    def scores(k):
        return jnp.concatenate(
            [jnp.dot(k, qT_ref[0, 0, g], preferred_element_type=F32) for g in range(n_qb)], axis=1)

    def weighted_values(first_block, p):
        vb = vT_ref.shape[-1]
        return sum(jnp.dot(vT_ref[0, 0, first_block + j], p[j * vb:(j + 1) * vb],
                           preferred_element_type=F32) for j in range(p.shape[0] // vb))

    def meta_block():
        s = scores(km_ref[0])
        row = lax.broadcasted_iota(jnp.int32, s.shape, 0)
        s = jnp.where(row < n_meta, s, MASK_VALUE)
        m = jnp.max(s, axis=0, keepdims=True)
        p = jnp.exp2(s - m)
        acc_ref[...] = jnp.dot(vTm_ref[0], p.astype(BF16), preferred_element_type=F32)
        return m, jnp.sum(p, axis=0, keepdims=True)

    def chunk(c, carry, *, two_pass):
        m, l, jump = carry
        off = pl.multiple_of(c * tk, tk)
        s = scores(k_ref[0, 0, pl.ds(off, tk), :])
        m_new = jnp.maximum(m, jnp.max(s, axis=0, keepdims=True))
        alpha = jnp.exp2(m - m_new)
        p = jnp.exp2(s - (m_new if two_pass else m))
        ls = jnp.sum(p, axis=0, keepdims=True)
        pv = weighted_values(c * kb_per_chunk, p.astype(BF16))
        if two_pass:
            acc_ref[...] = acc_ref[...] * alpha + pv
            l = l * alpha + ls
        else:
            acc_ref[...] = (acc_ref[...] + pv) * alpha
            l = (l + ls) * alpha
        return m_new, l, jnp.maximum(jump, m_new - m)

    def sweep(*, two_pass):
        m, l = meta_block()
        carry = (m, l, jnp.zeros_like(m))
        if two_pass:
            return lax.fori_loop(0, n_kv, functools.partial(chunk, two_pass=True), carry)

        def pair(jj, carry):
            carry = chunk(2 * jj, carry, two_pass=False)
            return chunk(2 * jj + 1, carry, two_pass=False)

        n_trips = n_kv // 2 + jnp.minimum(pl.program_id(2), 0)
        return lax.fori_loop(0, n_trips, pair, carry)

    _, l, jump = sweep(two_pass=False)
    l_ref[...] = l

    @pl.when(jnp.max(jump) > MAX_JUMP)
    def _():
        _, l, _ = sweep(two_pass=True)
        l_ref[...] = l

    o = (acc_ref[...] / l_ref[...]).T
    o_ref[0] = (o * sz_ref[0].astype(F32)).astype(o_ref.dtype)


def _flash(qT, k, vT, k_meta, vT_meta, sz, *, n_meta, tq_pref=2048, tk_pref=2048):
    bsz, n_heads, n_blocks, _, qb = qT.shape
    s = n_blocks * qb
    tq = _tile(s, tq_pref)
    tk = _tile(s // 2, tk_pref)
    return pl.pallas_call(
        functools.partial(_flash_kernel, tk=tk, n_meta=n_meta),
        out_shape=jax.ShapeDtypeStruct((bsz, s, n_heads * V_DIM), BF16),
        grid=(bsz, n_heads, s // tq),
        in_specs=[pl.BlockSpec((1, 1, tq // qb, QK_PAD, qb), lambda b, h, i: (b, h, i, 0, 0)),
                  pl.BlockSpec((1, 1, s, QK_PAD), lambda b, h, i: (b, h, 0, 0)),
                  pl.BlockSpec((1, 1, n_blocks, V_DIM, qb), lambda b, h, i: (b, h, 0, 0, 0)),
                  pl.BlockSpec((1, META_PAD, QK_PAD), lambda b, h, i: (h, 0, 0)),
                  pl.BlockSpec((1, V_DIM, META_PAD), lambda b, h, i: (h, 0, 0)),
                  pl.BlockSpec((1, tq, V_DIM), lambda b, h, i: (b, i, h))],
        out_specs=pl.BlockSpec((1, tq, V_DIM), lambda b, h, i: (b, i, h)),
        scratch_shapes=[pltpu.VMEM((V_DIM, tq), F32), pltpu.VMEM((1, tq), F32)],
        compiler_params=_params("parallel", "parallel", "arbitrary"),
        name="flash",
    )(qT, k, vT, k_meta, vT_meta, sz)


def _out_kernel(x_ref, oz_ref, u_ref, up_ref, un_ref, um_ref, bz_ref, ga_ref, gc_ref,
                cw_ref, wa_ref, wc_ref, wo_ref, fw_ref, y_ref, *, tiles_per_seq, n_meta):
    i = pl.program_id(0)
    tm = u_ref.shape[0]
    pos = i % tiles_per_seq
    u = u_ref[...].astype(F32)
    halo = up_ref.shape[0]
    prev_row = jnp.where(pos == 0,
                         um_ref[...].astype(F32)[n_meta - 1:n_meta],
                         up_ref[...].astype(F32)[halo - 1:halo])
    next_row = jnp.where(pos == tiles_per_seq - 1,
                         jnp.zeros((1, u.shape[1]), F32),
                         un_ref[...].astype(F32)[0:1])
    rid = lax.broadcasted_iota(jnp.int32, u.shape, 0)
    u_prev = jnp.where(rid == 0, prev_row, pltpu.roll(u, 1, 0))
    u_next = jnp.where(rid == tm - 1, next_row, pltpu.roll(u, tm - 1, 0))
    cw = cw_ref[...]
    conv = cw[0:1] * u_prev + cw[1:2] * u + cw[2:3] * u_next
    yc_in = (bz_ref[...].astype(F32) * conv).astype(BF16)
    y_conv = jnp.dot(yc_in, wc_ref[...], preferred_element_type=F32)
    y_attn = jnp.dot(oz_ref[...], wa_ref[...], preferred_element_type=F32)
    merged = ga_ref[...].astype(F32) * y_attn + gc_ref[...].astype(F32) * y_conv
    out = x_ref[...] + jnp.dot(merged.astype(BF16), wo_ref[...], preferred_element_type=F32)
    y_ref[...] = _rms(out, fw_ref[...])


def _out(x, oz, u, u_meta, bz, g, w, *, seq_len, n_meta, tm_pref=256):
    t, d = x.shape
    cw = u.shape[1]
    aw = oz.shape[1]
    tm = _tile(seq_len, tm_pref)
    halo = 16
    hb = tm // halo
    n_halo = t // halo
    row = lambda i: (i, 0)
    kern = functools.partial(_out_kernel, tiles_per_seq=seq_len // tm, n_meta=n_meta)
    return pl.pallas_call(
        kern,
        out_shape=jax.ShapeDtypeStruct((t, d), F32),
        grid=(t // tm,),
        in_specs=[pl.BlockSpec((tm, d), row),
                  pl.BlockSpec((tm, aw), row),
                  pl.BlockSpec((tm, cw), row),
                  pl.BlockSpec((halo, cw), lambda i: (jnp.maximum(i * hb - 1, 0), 0)),
                  pl.BlockSpec((halo, cw), lambda i: (jnp.minimum((i + 1) * hb, n_halo - 1), 0)),
                  pl.BlockSpec((halo, cw), lambda i: (0, 0)),
                  pl.BlockSpec((tm, cw), row),
                  pl.BlockSpec((tm, d), lambda i: (i, 0)),
                  pl.BlockSpec((tm, d), lambda i: (i, 1)),
                  _resident(w["conv_w"].shape),
                  _resident(w["w_o_attn"].shape),
                  _resident(w["w_o_conv"].shape),
                  _resident(w["w_o"].shape),
                  _resident((1, d))],
        out_specs=pl.BlockSpec((tm, d), row),
        compiler_params=_params("parallel"),
        name="out",
    )(x, oz, u, u, u, u_meta, bz, g, g, w["conv_w"], w["w_o_attn"], w["w_o_conv"], w["w_o"],
      w["final_norm"])


def _prep_weights(norm_w, w_in, b_gate, q_a_norm_w, w_uq, kv_a_norm_w, w_ukv,
                  w_o_attn, conv_w, w_o_conv, w_o, final_norm_w):
    d = w_in.shape[1]
    q_rank = q_a_norm_w.shape[-1]
    kv_rank = kv_a_norm_w.shape[-1]
    n_heads = w_uq.shape[-1] // QK_DIM
    aw = w_o_attn.shape[1]
    cwid = w_o_conv.shape[1]
    sizes = (q_rank, kv_rank, QK_ROPE_DIM, aw, cwid, cwid, cwid, cwid, 2 * d)
    assert sum(sizes) == w_in.shape[-1]
    cols, start = [], 0
    for sz in sizes:
        cols.append((start, start + sz))
        start += sz
    wt = jnp.transpose(w_in[0])
    n_small = cols[2][1]
    is_z_row = (jnp.arange(wt.shape[0] - n_small) < aw)[:, None]
    w_wide = (wt[n_small:] * jnp.where(is_z_row, norm_w[0][None, :], 1.0)).astype(BF16)
    cut = lambda k: (w_wide, cols[k][0] - n_small, cols[k][1] - cols[k][0])
    wuq = w_uq[0].reshape(q_rank, n_heads, QK_DIM).transpose(1, 2, 0)
    wukv = w_ukv[0].reshape(kv_rank, n_heads, QK_NOPE_DIM + V_DIM)
    return dict(
        n_heads=n_heads, q_rank=q_rank, kv_rank=kv_rank,
        norm=norm_w[0],
        w_small=jnp.transpose(wt[:n_small].astype(BF16)),
        w_z=cut(3), w_cx=cut(4), w_cb=cut(5), w_cc=cut(6), w_zc=cut(7), w_g=cut(8),
        b_gate=b_gate[0],
        q_norm=q_a_norm_w[0].reshape(1, q_rank),
        kv_norm=kv_a_norm_w[0].reshape(1, kv_rank),
        w_uqT=wuq.reshape(n_heads * QK_DIM, q_rank).astype(BF16),
        w_uk=wukv[:, :, :QK_NOPE_DIM].reshape(kv_rank, n_heads * QK_NOPE_DIM).astype(BF16),
        w_uvT=wukv[:, :, QK_NOPE_DIM:].transpose(1, 2, 0).reshape(n_heads * V_DIM, kv_rank).astype(BF16),
        w_o_attn=w_o_attn[0].astype(BF16),
        conv_w=conv_w[0],
        w_o_conv=w_o_conv[0].astype(BF16),
        w_o=w_o[0].astype(BF16),
        final_norm=final_norm_w.reshape(1, d),
    )


def _meta_state(meta_tokens, w):
    n_meta, d = meta_tokens.shape
    xm = jnp.pad(meta_tokens, ((0, META_PAD - n_meta), (0, 0)))
    xn, _ = _norm_silu(xm, w["w_z"], w["norm"])
    _, k_meta, vT_meta = _attn_prep(xn[None], w, _rope_tables(0, META_PAD))
    u_meta = _proj(xn, w["w_cx"], w["w_cc"], "mul")
    return k_meta[0], vT_meta[0, :, 0], u_meta


def _trunk(x, meta, w, n_meta):
    bsz, s, d = x.shape
    k_meta, vT_meta, u_meta = meta
    x2 = x.reshape(bsz * s, d)
    xn, sz = _norm_silu(x2, w["w_z"], w["norm"])
    qT, k, vT = _attn_prep(xn.reshape(bsz, s, d), w, _rope_tables(n_meta, s))
    u =_proj(xn, w["w_cx"], w["w_cc"], "mul")
    bz = _proj(xn, w["w_cb"], w["w_zc"], "mul_silu")
    g = _proj(xn, w["w_g"], w["b_gate"], "sigmoid_bias")
    oz = _flash(qT, k, vT, k_meta, vT_meta, sz.reshape(bsz, s, -1), n_meta=n_meta)
    y = _out(x2, oz.reshape(bsz * s, -1), u, u_meta, bz, g, w, seq_len=s, n_meta=n_meta)
    return y.reshape(bsz, s, d)


def kernel(x_prompt, x_sample, meta_tokens, norm_w, w_in, b_gate, q_a_norm_w, w_uq, kv_a_norm_w, w_ukv, w_o_attn, conv_w, w_o_conv, w_o, final_norm_w):
    assert norm_w.shape[0] == 1
    n_meta = meta_tokens.shape[0]
    assert n_meta <= 16
    w = _prep_weights(norm_w, w_in, b_gate, q_a_norm_w, w_uq, kv_a_norm_w, w_ukv,
                      w_o_attn, conv_w, w_o_conv, w_o, final_norm_w)
    meta = _meta_state(meta_tokens, w)
    return (_trunk(x_prompt, meta, w, n_meta), _trunk(x_sample, meta, w, n_meta))
```

```python
import functools

import jax
import jax.numpy as jnp
from jax import lax
from jax.experimental import pallas as pl
from jax.experimental.pallas import tpu as pltpu

QK_NOPE_DIM = 128
QK_ROPE_DIM = 64
QK_DIM = QK_NOPE_DIM + QK_ROPE_DIM
V_DIM = 128
ROPE_HALF = QK_ROPE_DIM // 2
QK_PAD = 256
ROPE_THETA = 10000.0
NORM_EPS = 1e-6
MASK_VALUE = -1e30
MAX_JUMP = 64.0
LOG2_E = 1.4426950408889634
ONES_ROWS = 16
META_PAD = 128
VMEM_LIMIT_BYTES = 56 * 1024 * 1024

F32 = jnp.float32
BF16 = jnp.bfloat16


def _params(*semantics):
    return pltpu.CompilerParams(dimension_semantics=semantics,
                                vmem_limit_bytes=VMEM_LIMIT_BYTES)


def _tile(n, pref):
    if n <= pref:
        return n
    t = pref
    while n % t:
        t //= 2
    return t


def _nt_dot(a, b):
    return lax.dot_general(a, b, (((1,), (1,)), ((), ())), preferred_element_type=F32)


def _rms(x, w):
    return x * lax.rsqrt(jnp.mean(x * x, axis=-1, keepdims=True) + NORM_EPS) * w


def _resident(shape):
    nd = len(shape)
    return pl.BlockSpec(shape, lambda *_: (0,) * nd, pipeline_mode=pl.Buffered(1))


def _norm_silu_kernel(x_ref, wz_ref, nw_ref, xn_ref, sz_ref):
    x = x_ref[...]
    r = lax.rsqrt(jnp.mean(x * x, axis=-1, keepdims=True) + NORM_EPS)
    z = _nt_dot(x.astype(BF16), wz_ref[...]) * r
    sz_ref[...] = jax.nn.silu(z).astype(sz_ref.dtype)

    @pl.when(pl.program_id(1) == 0)
    def _():
        xn_ref[...] = (x * r * nw_ref[...]).astype(xn_ref.dtype)


def _w_rows_spec(view, tn, d):
    _, first_row, n_rows = view
    assert first_row % tn == 0 and n_rows % tn == 0
    return pl.BlockSpec((tn, d), lambda i, j: (first_row // tn + j, 0))


def _norm_silu(x, wz_folded, norm_w, *, tm_pref=1024, tn_pref=1024):
    t, d = x.shape
    n = wz_folded[2]
    tm, tn = _tile(t, tm_pref), _tile(n, tn_pref)
    return pl.pallas_call(
        _norm_silu_kernel,
        out_shape=(jax.ShapeDtypeStruct((t, d), BF16), jax.ShapeDtypeStruct((t, n), BF16)),
        grid=(t // tm, n // tn),
        in_specs=[pl.BlockSpec((tm, d), lambda i, j: (i, 0)),
                  _w_rows_spec(wz_folded, tn, d),
                  pl.BlockSpec((1, d), lambda i, j: (0, 0))],
        out_specs=(pl.BlockSpec((tm, d), lambda i, j: (i, 0)),
                   pl.BlockSpec((tm, tn), lambda i, j: (i, j))),
        compiler_params=_params("parallel", "arbitrary"),
        name="norm_silu",
    )(x, wz_folded[0], norm_w.reshape(1, d))


def _attn_prep_kernel(xn_ref, wsm_ref, qnw_ref, kvnw_ref, wuqT_ref, wuk_ref, wuvT_ref,
                      cosT_ref, sinT_ref,
                      qT_ref, k_ref, vT_ref, *, n_heads, q_rank, kv_rank, scale, sub):
    tm = xn_ref.shape[1]
    q_zero = jnp.zeros((QK_PAD - QK_DIM, sub), BF16)
    k_zero = jnp.zeros((sub, QK_PAD - QK_DIM), BF16)
    r1 = QK_NOPE_DIM + ROPE_HALF

    def group(g, carry):
        t0 = pl.multiple_of(g * sub, sub)
        tok = pl.ds(t0, sub)
        p = jnp.dot(xn_ref[0, tok, :], wsm_ref[...], preferred_element_type=F32)
        qn = _rms(p[:, :q_rank], qnw_ref[...]).astype(BF16)
        cn = _rms(p[:, q_rank:q_rank + kv_rank], kvnw_ref[...]).astype(BF16)
        kr = p[:, q_rank + kv_rank:]
        kr_sw = jnp.concatenate([kr[:, ROPE_HALF:], kr[:, :ROPE_HALF]], axis=1)
        c = cosT_ref[:, tok]
        s = sinT_ref[:, tok]
        ck, sk = c.T, s.T
        kr_rot = (kr * jnp.concatenate([ck, ck], axis=1)
                  + kr_sw * jnp.concatenate([-sk, sk], axis=1)).astype(BF16)

        qT = _nt_dot(wuqT_ref[...], qn) * scale
        k_nope = jnp.dot(cn, wuk_ref[...], preferred_element_type=F32)
        vT = _nt_dot(wuvT_ref[...], cn)
        for h in range(n_heads):
            b = h * QK_DIM
            x1 = qT[b + QK_NOPE_DIM:b + r1]
            x2 = qT[b + r1:b + QK_DIM]
            qT_ref[0, h, 0, 0:QK_NOPE_DIM, tok] = qT[b:b + QK_NOPE_DIM].astype(BF16)
            qT_ref[0, h, 0, QK_NOPE_DIM:r1, tok] = (x1 * c - x2 * s).astype(BF16)
            qT_ref[0, h, 0, r1:QK_DIM, tok] = (x2 * c + x1 * s).astype(BF16)
            qT_ref[0, h, 0, QK_DIM:QK_PAD, tok] = q_zero
            k_ref[0, h, tok, 0:QK_NOPE_DIM] = k_nope[:, h * QK_NOPE_DIM:(h + 1) * QK_NOPE_DIM].astype(BF16)
            k_ref[0, h, tok, QK_NOPE_DIM:QK_DIM] = kr_rot
            k_ref[0, h, tok, QK_DIM:QK_PAD] = k_zero
            vT_ref[0, h, 0, :, tok] = vT[h * V_DIM:(h + 1) * V_DIM].astype(BF16)
        return carry

    lax.fori_loop(0, tm // sub, group, 0)


def _attn_prep(xn, w, rope, *, tm_pref=512, sub_pref=256):
    bsz, s, d = xn.shape
    n_heads = w["n_heads"]
    q_rank, kv_rank = w["q_rank"], w["kv_rank"]
    tm = _tile(s, tm_pref)
    sub = _tile(tm, sub_pref)
    cosT, sinT = rope
    kern = functools.partial(_attn_prep_kernel, n_heads=n_heads, q_rank=q_rank,
                             kv_rank=kv_rank, scale=QK_DIM ** -0.5 * LOG2_E, sub=sub)
    return pl.pallas_call(
        kern,
        out_shape=(jax.ShapeDtypeStruct((bsz, n_heads, s // tm, QK_PAD, tm), BF16),
                   jax.ShapeDtypeStruct((bsz, n_heads, s, QK_PAD), BF16),
                   jax.ShapeDtypeStruct((bsz, n_heads, s // tm, V_DIM, tm), BF16)),
        grid=(bsz, s // tm),
        in_specs=[pl.BlockSpec((1, tm, d), lambda b, i: (b, i, 0)),
                  _resident(w["w_small"].shape),
                  _resident((1, q_rank)),
                  _resident((1, kv_rank)),
                  _resident(w["w_uqT"].shape),
                  _resident(w["w_uk"].shape),
                  _resident(w["w_uvT"].shape),
                  pl.BlockSpec((ROPE_HALF, tm), lambda b, i: (0, i)),
                  pl.BlockSpec((ROPE_HALF, tm), lambda b, i: (0, i))],
        out_specs=(pl.BlockSpec((1, n_heads, 1, QK_PAD, tm), lambda b, i: (b, 0, i, 0, 0)),
                   pl.BlockSpec((1, n_heads, tm, QK_PAD), lambda b, i: (b, 0, i, 0)),
                   pl.BlockSpec((1, n_heads, 1, V_DIM, tm), lambda b, i: (b, 0, i, 0, 0))),
        compiler_params=_params("parallel", "parallel"),
        name="attn_prep",
    )(xn, w["w_small"], w["q_norm"], w["kv_norm"], w["w_uqT"], w["w_uk"], w["w_uvT"],
      cosT, sinT)


def _rope_tables(start, length):
    inv_freq = 1.0 / (ROPE_THETA ** (jnp.arange(0, QK_ROPE_DIM, 2, dtype=F32) / QK_ROPE_DIM))
    pos = jnp.arange(start, start + length, dtype=F32)
    ang = pos[:, None] * inv_freq[None, :]
    return jnp.cos(ang).T, jnp.sin(ang).T


def _proj_kernel(*refs, mode):
    xn_ref, w1_ref = refs[0], refs[1]
    o_ref = refs[-1]
    xn = xn_ref[...]
    p1 = _nt_dot(xn, w1_ref[...])
    if mode == "sigmoid_bias":
        out = jax.nn.sigmoid(p1 + refs[2][...])
    else:
        p2 = _nt_dot(xn, refs[2][...])
        out = p1 * p2 if mode == "mul" else p1 * jax.nn.silu(p2)
    o_ref[...] = out.astype(o_ref.dtype)


def _proj(xn, w1, second, mode, *, tm_pref=1024, tn_pref=1024):
    t, d = xn.shape
    n = w1[2]
    single = mode == "sigmoid_bias"
    tm, tn = _tile(t, tm_pref), _tile(n, 2 * tn_pref if single else tn_pref)
    in_specs = [pl.BlockSpec((tm, d), lambda i, j: (i, 0)), _w_rows_spec(w1, tn, d)]
    args = [xn, w1[0]]
    if single:
        in_specs.append(pl.BlockSpec((1, tn), lambda i, j: (0, j)))
        args.append(second.reshape(1, n))
    else:
        in_specs.append(_w_rows_spec(second, tn, d))
        args.append(second[0])
    return pl.pallas_call(
        functools.partial(_proj_kernel, mode=mode),
        out_shape=jax.ShapeDtypeStruct((t, n), BF16),
        grid=(t // tm, n // tn),
        in_specs=in_specs,
        out_specs=pl.BlockSpec((tm, tn), lambda i, j: (i, j)),
        compiler_params=_params("parallel", "parallel"),
        name="proj_" + mode,
    )(*args)


def _flash_kernel(qT_ref, k_ref, vT_ref, km_ref, vTm_ref, sz_ref, o_ref, acc_ref, l_ref,
                  *, tk, n_meta):
    n_qb, _, qb = qT_ref.shape[2:]
    n_kv = k_ref.shape[2] // tk
    kb_per_chunk = tk // vT_ref.shape[-1]

    def scores(k):
        return jnp.concatenate(
            [jnp.dot(k, qT_ref[0, 0, g], preferred_element_type=F32) for g in range(n_qb)], axis=1)

    def weighted_values(first_block, p):
        vb = vT_ref.shape[-1]
        ones = jnp.ones((ONES_ROWS, vb), BF16)
        ext = sum(jnp.dot(jnp.concatenate([vT_ref[0, 0, first_block + j], ones], axis=0),
                          p[j * vb:(j + 1) * vb], preferred_element_type=F32)
                  for j in range(p.shape[0] // vb))
        return ext[:V_DIM], ext[V_DIM:V_DIM + 1]

    def meta_block():
        s = scores(km_ref[0])
        row = lax.broadcasted_iota(jnp.int32, s.shape, 0)
        s = jnp.where(row < n_meta, s, MASK_VALUE)
        m = jnp.max(s, axis=0, keepdims=True)
        p = jnp.exp2(s - m)
        acc_ref[...] = jnp.dot(vTm_ref[0], p.astype(BF16), preferred_element_type=F32)
        return m, jnp.sum(p, axis=0, keepdims=True)

    def chunk(c, carry, *, two_pass):
        m, l, jump = carry
        off = pl.multiple_of(c * tk, tk)
        s = scores(k_ref[0, 0, pl.ds(off, tk), :])
        m_new = jnp.maximum(m, jnp.max(s, axis=0, keepdims=True))
        alpha = jnp.exp2(m - m_new)
        p = jnp.exp2(s - (m_new if two_pass else m))
        pv, ls = weighted_values(c * kb_per_chunk, p.astype(BF16))
        if two_pass:
            acc_ref[...] = acc_ref[...] * alpha + pv
            l = l * alpha + ls
        else:
            acc_ref[...] = (acc_ref[...] + pv) * alpha
            l = (l + ls) * alpha
        return m_new, l, jnp.maximum(jump, m_new - m)

    def sweep(*, two_pass):
        m, l = meta_block()
        carry = (m, l, jnp.zeros_like(m))
        if two_pass:
            return lax.fori_loop(0, n_kv, functools.partial(chunk, two_pass=True), carry)

        def pair(jj, carry):
            carry = chunk(2 * jj, carry, two_pass=False)
            return chunk(2 * jj + 1, carry, two_pass=False)

        n_trips = n_kv // 2 + jnp.minimum(pl.program_id(2), 0)
        return lax.fori_loop(0, n_trips, pair, carry)

    _, l, jump = sweep(two_pass=False)
    l_ref[...] = l

    @pl.when(jnp.max(jump) > MAX_JUMP)
    def _():
        _, l, _ = sweep(two_pass=True)
        l_ref[...] = l

    o = (acc_ref[...] / l_ref[...]).T
    o_ref[0] = (o * sz_ref[0].astype(F32)).astype(o_ref.dtype)


def _flash(qT, k, vT, k_meta, vT_meta, sz, *, n_meta, tq_pref=2048, tk_pref=2048):
    bsz, n_heads, n_blocks, _, qb = qT.shape
    s = n_blocks * qb
    tq = _tile(s, tq_pref)
    tk = _tile(s // 2, tk_pref)
    return pl.pallas_call(
        functools.partial(_flash_kernel, tk=tk, n_meta=n_meta),
        out_shape=jax.ShapeDtypeStruct((bsz, s, n_heads * V_DIM), BF16),
        grid=(bsz, n_heads, s // tq),
        in_specs=[pl.BlockSpec((1, 1, tq // qb, QK_PAD, qb), lambda b, h, i: (b, h, i, 0, 0)),
                  pl.BlockSpec((1, 1, s, QK_PAD), lambda b, h, i: (b, h, 0, 0)),
                  pl.BlockSpec((1, 1, n_blocks, V_DIM, qb), lambda b, h, i: (b, h, 0, 0, 0)),
                  pl.BlockSpec((1, META_PAD, QK_PAD), lambda b, h, i: (h, 0, 0)),
                  pl.BlockSpec((1, V_DIM, META_PAD), lambda b, h, i: (h, 0, 0)),
                  pl.BlockSpec((1, tq, V_DIM), lambda b, h, i: (b, i, h))],
        out_specs=pl.BlockSpec((1, tq, V_DIM), lambda b, h, i: (b, i, h)),
        scratch_shapes=[pltpu.VMEM((V_DIM, tq), F32), pltpu.VMEM((1, tq), F32)],
        compiler_params=_params("parallel", "parallel", "arbitrary"),
        name="flash",
    )(qT, k, vT, k_meta, vT_meta, sz)


def _out_kernel(x_ref, oz_ref, u_ref, up_ref, un_ref, um_ref, bz_ref, ga_ref, gc_ref,
                cw_ref, wa_ref, wc_ref, wo_ref, fw_ref, y_ref, *, tiles_per_seq, n_meta):
    i = pl.program_id(0)
    tm = u_ref.shape[0]
    pos = i % tiles_per_seq
    u = u_ref[...].astype(F32)
    halo = up_ref.shape[0]
    prev_row = jnp.where(pos == 0,
                         um_ref[...].astype(F32)[n_meta - 1:n_meta],
                         up_ref[...].astype(F32)[halo - 1:halo])
    next_row = jnp.where(pos == tiles_per_seq - 1,
                         jnp.zeros((1, u.shape[1]), F32),
                         un_ref[...].astype(F32)[0:1])
    rid = lax.broadcasted_iota(jnp.int32, u.shape, 0)
    u_prev = jnp.where(rid == 0, prev_row, pltpu.roll(u, 1, 0))
    u_next = jnp.where(rid == tm - 1, next_row, pltpu.roll(u, tm - 1, 0))
    cw = cw_ref[...]
    conv = cw[0:1] * u_prev + cw[1:2] * u + cw[2:3] * u_next
    yc_in = (bz_ref[...].astype(F32) * conv).astype(BF16)
    y_conv = jnp.dot(yc_in, wc_ref[...], preferred_element_type=F32)
    y_attn = jnp.dot(oz_ref[...], wa_ref[...], preferred_element_type=F32)
    merged = ga_ref[...].astype(F32) * y_attn + gc_ref[...].astype(F32) * y_conv
    out = x_ref[...] + jnp.dot(merged.astype(BF16), wo_ref[...], preferred_element_type=F32)
    y_ref[...] = _rms(out, fw_ref[...])


def _out(x, oz, u, u_meta, bz, g, w, *, seq_len, n_meta, tm_pref=256):
    t, d = x.shape
    cw = u.shape[1]
    aw = oz.shape[1]
    tm = _tile(seq_len, tm_pref)
    halo = 16
    hb = tm // halo
    n_halo = t // halo
    row = lambda i: (i, 0)
    kern = functools.partial(_out_kernel, tiles_per_seq=seq_len // tm, n_meta=n_meta)
    return pl.pallas_call(
        kern,
        out_shape=jax.ShapeDtypeStruct((t, d), F32),
        grid=(t // tm,),
        in_specs=[pl.BlockSpec((tm, d), row),
                  pl.BlockSpec((tm, aw), row),
                  pl.BlockSpec((tm, cw), row),
                  pl.BlockSpec((halo, cw), lambda i: (jnp.maximum(i * hb - 1, 0), 0)),
                  pl.BlockSpec((halo, cw), lambda i: (jnp.minimum((i + 1) * hb, n_halo - 1), 0)),
                  pl.BlockSpec((halo, cw), lambda i: (0, 0)),
                  pl.BlockSpec((tm, cw), row),
                  pl.BlockSpec((tm, d), lambda i: (i, 0)),
                  pl.BlockSpec((tm, d), lambda i: (i, 1)),
                  _resident(w["conv_w"].shape),
                  _resident(w["w_o_attn"].shape),
                  _resident(w["w_o_conv"].shape),
                  _resident(w["w_o"].shape),
                  _resident((1, d))],
        out_specs=pl.BlockSpec((tm, d), row),
        compiler_params=_params("parallel"),
        name="out",
    )(x, oz, u, u, u, u_meta, bz, g, g, w["conv_w"], w["w_o_attn"], w["w_o_conv"], w["w_o"],
      w["final_norm"])


def _prep_weights(norm_w, w_in, b_gate, q_a_norm_w, w_uq, kv_a_norm_w, w_ukv,
                  w_o_attn, conv_w, w_o_conv, w_o, final_norm_w):
    d = w_in.shape[1]
    q_rank = q_a_norm_w.shape[-1]
    kv_rank = kv_a_norm_w.shape[-1]
    n_heads = w_uq.shape[-1] // QK_DIM
    aw = w_o_attn.shape[1]
    cwid = w_o_conv.shape[1]
    sizes = (q_rank, kv_rank, QK_ROPE_DIM, aw, cwid, cwid, cwid, cwid, 2 * d)
    assert sum(sizes) == w_in.shape[-1]
    cols, start = [], 0
    for sz in sizes:
        cols.append((start, start + sz))
        start += sz
    wt = jnp.transpose(w_in[0])
    n_small = cols[2][1]
    is_z_row = (jnp.arange(wt.shape[0] - n_small) < aw)[:, None]
    w_wide = (wt[n_small:] * jnp.where(is_z_row, norm_w[0][None, :], 1.0)).astype(BF16)
    cut = lambda k: (w_wide, cols[k][0] - n_small, cols[k][1] - cols[k][0])
    wuq = w_uq[0].reshape(q_rank, n_heads, QK_DIM).transpose(1, 2, 0)
    wukv = w_ukv[0].reshape(kv_rank, n_heads, QK_NOPE_DIM + V_DIM)
    return dict(
        n_heads=n_heads, q_rank=q_rank, kv_rank=kv_rank,
        norm=norm_w[0],
        w_small=jnp.transpose(wt[:n_small].astype(BF16)),
        w_z=cut(3), w_cx=cut(4), w_cb=cut(5), w_cc=cut(6), w_zc=cut(7), w_g=cut(8),
        b_gate=b_gate[0],
        q_norm=q_a_norm_w[0].reshape(1, q_rank),
        kv_norm=kv_a_norm_w[0].reshape(1, kv_rank),
        w_uqT=wuq.reshape(n_heads * QK_DIM, q_rank).astype(BF16),
        w_uk=wukv[:, :, :QK_NOPE_DIM].reshape(kv_rank, n_heads * QK_NOPE_DIM).astype(BF16),
        w_uvT=wukv[:, :, QK_NOPE_DIM:].transpose(1, 2, 0).reshape(n_heads * V_DIM, kv_rank).astype(BF16),
        w_o_attn=w_o_attn[0].astype(BF16),
        conv_w=conv_w[0],
        w_o_conv=w_o_conv[0].astype(BF16),
        w_o=w_o[0].astype(BF16),
        final_norm=final_norm_w.reshape(1, d),
    )


def _meta_state(meta_tokens, w):
    n_meta, d = meta_tokens.shape
    xm = jnp.pad(meta_tokens, ((0, META_PAD - n_meta), (0, 0)))
    xn, _ = _norm_silu(xm, w["w_z"], w["norm"])
    _, k_meta, vT_meta = _attn_prep(xn[None], w, _rope_tables(0, META_PAD))
    u_meta = _proj(xn, w["w_cx"], w["w_cc"], "mul")
    return k_meta[0], vT_meta[0, :, 0], u_meta


def _trunk(x, meta, w, n_meta):
    bsz, s, d = x.shape
    k_meta, vT_meta, u_meta = meta
    x2 = x.reshape(bsz * s, d)
    xn, sz = _norm_silu(x2, w["w_z"], w["norm"])
    qT, k, vT = _attn_prep(xn.reshape(bsz, s, d), w, _rope_tables(n_meta, s))
    u =_proj(xn, w["w_cx"], w["w_cc"], "mul")
    bz = _proj(xn, w["w_cb"], w["w_zc"], "mul_silu")
    g = _proj(xn, w["w_g"], w["b_gate"], "sigmoid_bias")
    oz = _flash(qT, k, vT, k_meta, vT_meta, sz.reshape(bsz, s, -1), n_meta=n_meta)
    y = _out(x2, oz.reshape(bsz * s, -1), u, u_meta, bz, g, w, seq_len=s, n_meta=n_meta)
    return y.reshape(bsz, s, d)


def kernel(x_prompt, x_sample, meta_tokens, norm_w, w_in, b_gate, q_a_norm_w, w_uq, kv_a_norm_w, w_ukv, w_o_attn, conv_w, w_o_conv, w_o, final_norm_w):
    assert norm_w.shape[0] == 1
    n_meta = meta_tokens.shape[0]
    assert n_meta <= 16
    w = _prep_weights(norm_w, w_in, b_gate, q_a_norm_w, w_uq, kv_a_norm_w, w_ukv,
                      w_o_attn, conv_w, w_o_conv, w_o, final_norm_w)
    meta = _meta_state(meta_tokens, w)
    return (_trunk(x_prompt, meta, w, n_meta), _trunk(x_sample, meta, w, n_meta))
```
